```python
import math
import jax, jax.numpy as jnp
from jax import lax
import numpy as np

D_MODEL = 1024
BATCH = 8
SEQ = 2048
DEPTH = 1

CHUNK = 64
N_META = 16
Q_BLOCK = 128
RMS_EPS = 1e-6

D_MIX = D_MODEL
MLA_HEADS = 4
MLA_NOPE = 128
MLA_ROPE = 64
MLA_V = 128
MLA_Q_RANK = 256
MLA_KV_RANK = 128
ROPE_THETA = 10000.0
D_MLA = MLA_HEADS * MLA_V
D_MLA_IN = MLA_Q_RANK + MLA_KV_RANK + MLA_ROPE
RWKV_HEAD = 64
D_RWKV = D_MIX - D_MLA
RWKV_HEADS = D_RWKV // RWKV_HEAD
DECAY_RANK = 64
AAA_RANK = 64
GATE_RANK = 128
GN_EPS = 64e-5
D_RWKV_IN = 3 * D_RWKV + DECAY_RANK + AAA_RANK + GATE_RANK
D_IN = D_MLA_IN + D_RWKV_IN

PEER_HEADS = 8
N_KEYS = 128
N_EXPERTS = N_KEYS * N_KEYS
PEER_KEY_DIM = 256
PEER_HALF = PEER_KEY_DIM // 2
PEER_TOPK = 16
PEER_TOKEN_BLOCK = 128

kernel_name = "hymba_mla_rwkv7_peer_streaming"


def rmsnorm(x, g, eps=RMS_EPS):
    xf = x.astype(jnp.float32)
    y = xf * lax.rsqrt(jnp.mean(xf * xf, axis=-1, keepdims=True) + eps)
    return (y * g.astype(jnp.float32)).astype(x.dtype)


def chunk_ids(length):
    pos = jnp.arange(length)
    return jnp.where(pos < N_META, 0, (pos - N_META) // CHUNK + 1)


def rope_tables(length):
    half = MLA_ROPE // 2
    inv = ROPE_THETA ** (-jnp.arange(half, dtype=jnp.float32) / half)
    ang = jnp.arange(length, dtype=jnp.float32)[:, None] * inv[None, :]
    return jnp.cos(ang), jnp.sin(ang)


def apply_rope(x, cos, sin):
    half = x.shape[-1] // 2
    x1, x2 = x[..., :half], x[..., half:]
    c, s = cos.astype(x.dtype), sin.astype(x.dtype)
    return jnp.concatenate([x1 * c - x2 * s, x1 * s + x2 * c], axis=-1)


def mla_group(p_q, p_kv, p_kr, q_norm_g, w_uq, kv_norm_g, w_ukv, out_g):
    B, L, _ = p_q.shape
    q = (rmsnorm(p_q, q_norm_g) @ w_uq).reshape(B, L, MLA_HEADS, MLA_NOPE + MLA_ROPE).transpose(0, 2, 1, 3)
    kv = (rmsnorm(p_kv, kv_norm_g) @ w_ukv).reshape(B, L, MLA_HEADS, MLA_NOPE + MLA_V).transpose(0, 2, 1, 3)
    k_nope, v = kv[..., :MLA_NOPE], kv[..., MLA_NOPE:]
    cos, sin = rope_tables(L)
    q_nope = q[..., :MLA_NOPE]
    q_rope = apply_rope(q[..., MLA_NOPE:], cos, sin)
    k_rope = apply_rope(p_kr, cos, sin)

    n_blk = -(-L // Q_BLOCK)
    L_pad = n_blk * Q_BLOCK
    pad = L_pad - L
    cid = chunk_ids(L_pad)
    k_cid = cid[:L]

    def to_blocks(t):
        t = jnp.pad(t, ((0, 0), (0, 0), (0, pad), (0, 0)))
        return t.reshape(B, MLA_HEADS, n_blk, Q_BLOCK, t.shape[-1]).transpose(2, 0, 1, 3, 4)

    scale = (MLA_NOPE + MLA_ROPE) ** -0.5

    def attend(blk):
        qn, qr, qc = blk
        s = jnp.einsum('bhqd,bhkd->bhqk', qn, k_nope) + jnp.einsum('bhqr,bkr->bhqk', qr, k_rope)
        s = s.astype(jnp.float32) * scale
        visible = k_cid[None, :] <= qc[:, None]
        s = jnp.where(visible, s, -jnp.inf)
        p = jax.nn.softmax(s, axis=-1).astype(v.dtype)
        return jnp.einsum('bhqk,bhkv->bhqv', p, v)

    o = lax.map(attend, (to_blocks(q_nope), to_blocks(q_rope), cid.reshape(n_blk, Q_BLOCK)))
    o = o.transpose(1, 0, 3, 2, 4).reshape(B, L_pad, D_MLA)[:, :L]
    return rmsnorm(o, out_g)


def token_shift(p):
    return jnp.pad(p, ((0, 0), (1, 0), (0, 0)))[:, :-1]


def rwkv_group(p, mu, w0, w_up, a0, a_up, g_up, k_k, k_a, r_k, ln_w, ln_b):
    B, L, _ = p.shape
    dt = p.dtype
    p = p + mu * (token_shift(p) - p)
    i1 = D_RWKV
    i2 = 2 * D_RWKV
    i3 = 3 * D_RWKV
    i4 = i3 + DECAY_RANK
    i5 = i4 + AAA_RANK
    r, k, v, dw, da, dg = jnp.split(p, [i1, i2, i3, i4, i5], axis=-1)
    w = -jax.nn.softplus(-(w0 + jnp.tanh(dw) @ w_up)) - 0.5
    decay = jnp.exp(-jnp.exp(w.astype(jnp.float32)))
    a = jax.nn.sigmoid(a0 + da @ a_up)
    g = jax.nn.sigmoid(dg) @ g_up

    heads = lambda t: t.reshape(B, L, RWKV_HEADS, RWKV_HEAD)
    kk = heads(k * k_k).astype(jnp.float32)
    kk = kk / jnp.maximum(jnp.sqrt(jnp.sum(kk * kk, axis=-1, keepdims=True)), 1e-12)
    k = k * (1.0 + (a - 1.0) * k_a)
    r_h, k_h, v_h, a_h, w_h = heads(r), heads(k), heads(v), heads(a), heads(decay)
    b_h = kk * a_h

    def step(S, inp):
        r_t, w_t, k_t, v_t, kk_t, b_t = inp
        sa = jnp.einsum('bhij,bhj->bhi', S, -kk_t)
        S = S * w_t[:, :, None, :] + sa[..., :, None] * b_t[:, :, None, :] + v_t[..., :, None] * k_t[:, :, None, :]
        return S, jnp.einsum('bhij,bhj->bhi', S, r_t)

    xs = tuple(t.astype(jnp.float32).transpose(1, 0, 2, 3) for t in (r_h, w_h, k_h, v_h, kk, b_h))
    S0 = jnp.zeros((B, RWKV_HEADS, RWKV_HEAD, RWKV_HEAD), jnp.float32)
    _, o = lax.scan(step, S0, xs)
    o = o.transpose(1, 0, 2, 3)
    mean = jnp.mean(o, axis=-1, keepdims=True)
    var = jnp.mean(jnp.square(o - mean), axis=-1, keepdims=True)
    o = ((o - mean) * lax.rsqrt(var + GN_EPS)).reshape(B, L, D_RWKV)
    o = o * ln_w.astype(jnp.float32) + ln_b.astype(jnp.float32)
    bonus = jnp.sum(r_h * k_h * r_k, axis=-1, keepdims=True) * v_h
    o = o + bonus.reshape(B, L, D_RWKV).astype(jnp.float32)
    return (o * g.astype(jnp.float32)).astype(dt)


def peer_ffn(x, w_q, sub_keys, u, v):
    B, L, D = x.shape
    T = B * L
    t = x.reshape(T, D)
    q = (t @ w_q).reshape(T, PEER_HEADS, 2, PEER_HALF)
    s = jnp.einsum('thcd,hcnd->thcn', q, sub_keys)
    s1, i1 = lax.top_k(s[:, :, 0], PEER_TOPK)
    s2, i2 = lax.top_k(s[:, :, 1], PEER_TOPK)
    cand_s = (s1[..., :, None] + s2[..., None, :]).reshape(T, PEER_HEADS, PEER_TOPK * PEER_TOPK)
    cand_i = (i1[..., :, None] * N_KEYS + i2[..., None, :]).reshape(T, PEER_HEADS, PEER_TOPK * PEER_TOPK)
    top_s, sel = lax.top_k(cand_s, PEER_TOPK)
    idx = jnp.take_along_axis(cand_i, sel, axis=-1)
    gate = jax.nn.softmax(top_s.astype(jnp.float32), axis=-1).astype(x.dtype)

    n_blk = -(-T // PEER_TOKEN_BLOCK)
    pad = n_blk * PEER_TOKEN_BLOCK - T
    tb = jnp.pad(t, ((0, pad), (0, 0))).reshape(n_blk, PEER_TOKEN_BLOCK, D)
    ib = jnp.pad(idx, ((0, pad), (0, 0), (0, 0))).reshape(n_blk, PEER_TOKEN_BLOCK, PEER_HEADS, PEER_TOPK)
    gb = jnp.pad(gate, ((0, pad), (0, 0), (0, 0))).reshape(n_blk, PEER_TOKEN_BLOCK, PEER_HEADS, PEER_TOPK)

    def expert_block(blk):
        xt, it, gt = blk
        u_sel = jnp.take(u, it, axis=0)
        act = jax.nn.gelu(jnp.einsum('td,thkd->thk', xt, u_sel), approximate=False) * gt
        return jnp.einsum('thk,thkd->td', act, jnp.take(v, it, axis=0))

    out = lax.map(expert_block, (tb, ib, gb)).reshape(-1, D)[:T]
    return out.reshape(B, L, D)


def setup_inputs(seed: int = 0) -> dict:
    key = jax.random.key(seed)
    ks = jax.random.split(key, 27)
    f32 = jnp.float32
    nrm = lambda k, shape, sc: jax.random.normal(k, shape, f32) * sc
    gain = lambda k, shape: 1.0 + 0.01 * jax.random.normal(k, shape, f32)
    Ld = DEPTH
    return {
        "x": nrm(ks[0], (BATCH, SEQ, D_MODEL), 1.0),
        "meta_tokens": nrm(ks[1], (N_META, D_MODEL), 1.0),
        "norm_mix_g": gain(ks[2], (Ld, D_MODEL)),
        "w_in": nrm(ks[3], (Ld, D_MODEL, D_IN), D_MODEL ** -0.5),
        "mla_q_norm_g": gain(ks[4], (Ld, MLA_Q_RANK)),
        "mla_w_uq": nrm(ks[5], (Ld, MLA_Q_RANK, MLA_HEADS * (MLA_NOPE + MLA_ROPE)), MLA_Q_RANK ** -0.5),
        "mla_kv_norm_g": gain(ks[6], (Ld, MLA_KV_RANK)),
        "mla_w_ukv": nrm(ks[7], (Ld, MLA_KV_RANK, MLA_HEADS * (MLA_NOPE + MLA_V)), MLA_KV_RANK ** -0.5),
        "mla_out_g": gain(ks[8], (Ld, D_MLA)),
        "rwkv_mu": jax.random.uniform(ks[9], (Ld, D_RWKV_IN), f32, 0.0, 1.0),
        "rwkv_w0": jax.random.uniform(ks[10], (Ld, D_RWKV), f32, -6.0, 0.0),
        "rwkv_w_up": nrm(ks[11], (Ld, DECAY_RANK, D_RWKV), 0.1),
        "rwkv_a0": nrm(ks[12], (Ld, D_RWKV), 0.1),
        "rwkv_a_up": nrm(ks[13], (Ld, AAA_RANK, D_RWKV), 0.1),
        "rwkv_g_up": nrm(ks[14], (Ld, GATE_RANK, D_RWKV), GATE_RANK ** -0.5),
        "rwkv_k_k": 0.85 + nrm(ks[15], (Ld, D_RWKV), 0.02),
        "rwkv_k_a": 1.0 + nrm(ks[16], (Ld, D_RWKV), 0.02),
        "rwkv_r_k": nrm(ks[17], (Ld, RWKV_HEADS, RWKV_HEAD), 0.1),
        "rwkv_ln_w": gain(ks[18], (Ld, D_RWKV)),
        "rwkv_ln_b": nrm(ks[19], (Ld, D_RWKV), 0.01),
        "w_out": nrm(ks[20], (Ld, D_MIX, D_MODEL), D_MIX ** -0.5),
        "norm_ffn_g": gain(ks[21], (Ld, D_MODEL)),
        "peer_w_q": nrm(ks[22], (Ld, D_MODEL, PEER_HEADS * PEER_KEY_DIM), D_MODEL ** -0.5),
        "peer_sub_keys": nrm(ks[23], (Ld, PEER_HEADS, 2, N_KEYS, PEER_HALF), PEER_HALF ** -0.5),
        "peer_u": nrm(ks[24], (Ld, N_EXPERTS, D_MODEL), D_MODEL ** -0.5),
        "peer_v": nrm(ks[25], (Ld, N_EXPERTS, D_MODEL), 0.5),
        "norm_final_g": gain(ks[26], (D_MODEL,)),
    }


def reference(x, meta_tokens, norm_mix_g, w_in, mla_q_norm_g, mla_w_uq, mla_kv_norm_g, mla_w_ukv,
              mla_out_g, rwkv_mu, rwkv_w0, rwkv_w_up, rwkv_a0, rwkv_a_up, rwkv_g_up, rwkv_k_k, rwkv_k_a,
              rwkv_r_k, rwkv_ln_w, rwkv_ln_b, w_out, norm_ffn_g, peer_w_q, peer_sub_keys, peer_u, peer_v,
              norm_final_g):
    B = x.shape[0]
    meta = jnp.broadcast_to(meta_tokens[None].astype(x.dtype), (B, N_META, D_MODEL))
    h = jnp.concatenate([meta, x], axis=1)
    for l in range(DEPTH):
        n = rmsnorm(h, norm_mix_g[l])
        proj = n @ w_in[l]
        p_q, p_kv, p_kr, p_rwkv = jnp.split(
            proj, [MLA_Q_RANK, MLA_Q_RANK + MLA_KV_RANK, D_MLA_IN], axis=-1)
        y_mla = mla_group(p_q, p_kv, p_kr, mla_q_norm_g[l], mla_w_uq[l], mla_kv_norm_g[l],
                          mla_w_ukv[l], mla_out_g[l])
        y_rwkv = rwkv_group(p_rwkv, rwkv_mu[l], rwkv_w0[l], rwkv_w_up[l], rwkv_a0[l], rwkv_a_up[l],
                            rwkv_g_up[l], rwkv_k_k[l], rwkv_k_a[l], rwkv_r_k[l], rwkv_ln_w[l], rwkv_ln_b[l])
        h = h + jnp.concatenate([y_mla, y_rwkv], axis=-1) @ w_out[l]
        h = h + peer_ffn(rmsnorm(h, norm_ffn_g[l]), peer_w_q[l], peer_sub_keys[l], peer_u[l], peer_v[l])
    h = rmsnorm(h, norm_final_g)
    return h[:, N_META:]
```

```python
import functools
import math

import jax
import jax.numpy as jnp
from jax import lax
from jax.experimental import pallas as pl
from jax.experimental.pallas import tpu as pltpu

F32 = jnp.float32
BF16 = jnp.bfloat16
HIGHEST = lax.Precision.HIGHEST

RMS_EPS = 1e-6
GN_EPS = 64e-5
ROPE_THETA = 10000.0
N_META = 16
CHUNK = 64
X0 = 512
META0 = X0 - N_META
K0 = X0 - 128
SCAN_C0 = (X0 - CHUNK) // CHUNK
MLA_HEADS = 4
RWKV_HEAD = 64
PEER_TOPK = 16
VMEM_LIMIT = 56 * 1024 * 1024


def _dot(a, b, precision=None):
    return jnp.dot(a, b, preferred_element_type=F32, precision=precision)


def _dot_nt(a, b, precision=None):
    return lax.dot_general(a, b, (((1,), (1,)), ((), ())),
                           preferred_element_type=F32, precision=precision)


def _rms(x, g):
    return x * lax.rsqrt(jnp.mean(x * x, axis=-1, keepdims=True) + RMS_EPS) * g


def _params(*sem):
    return pltpu.CompilerParams(dimension_semantics=sem, vmem_limit_bytes=VMEM_LIMIT)


def _full(shape):
    n = len(shape)
    return pl.BlockSpec(shape, lambda *_: (0,) * n)


def _inproj_kernel(h_ref, g_ref, w_ref, pm_ref, pr_ref, *, n_mla):
    n = _rms(h_ref[...], g_ref[...]).astype(BF16)
    p = _dot(n, w_ref[...])
    pm_ref[...] = p[:, :n_mla]
    pr_ref[...] = p[:, n_mla:]


def _inproj(h, g, w, n_mla, tm):
    tp, d = h.shape
    n_all = w.shape[1]
    return pl.pallas_call(
        functools.partial(_inproj_kernel, n_mla=n_mla),
        grid=(tp // tm,),
        in_specs=[pl.BlockSpec((tm, d), lambda i: (i, 0)), _full((1, d)), _full(w.shape)],
        out_specs=[pl.BlockSpec((tm, n_mla), lambda i: (i, 0)),
                   pl.BlockSpec((tm, n_all - n_mla), lambda i: (i, 0))],
        out_shape=[jax.ShapeDtypeStruct((tp, n_mla), F32),
                   jax.ShapeDtypeStruct((tp, n_all - n_mla), F32)],
        compiler_params=_params("parallel"),
        name="inproj",
    )(h, g, w)


def _mla_prep_kernel(pm_ref, cos_ref, sin_ref, gq_ref, gkv_ref, wqa_ref, wqb_ref, wk_ref, wv_ref,
                     q_ref, k_ref, v_ref, *, scale):
    pm = pm_ref[...]
    nq = _rms(pm[:, 0:256], gq_ref[...]).astype(BF16)
    nkv = _rms(pm[:, 256:384], gkv_ref[...]).astype(BF16)
    cos, sin = cos_ref[...], sin_ref[...]
    qa = _dot(nq, wqa_ref[...])
    qb = _dot(nq, wqb_ref[...])
    kn = _dot(nkv, wk_ref[...])
    vv = _dot(nkv, wv_ref[...])
    kr = pm[:, 384:512] * cos + pm[:, 512:640] * sin
    for h in range(MLA_HEADS):
        qn = qa[:, 256 * h:256 * h + 128]
        qr = qa[:, 256 * h + 128:256 * h + 256] * cos + qb[:, 128 * h:128 * h + 128] * sin
        q_ref[h] = (jnp.concatenate([qn, qr], axis=-1) * scale).astype(BF16)
        k_ref[h] = jnp.concatenate([kn[:, 128 * h:128 * h + 128], kr], axis=-1).astype(BF16)
        v_ref[h] = vv[:, 128 * h:128 * h + 128].astype(BF16)


def _mla_prep(pm, cos, sin, gq, gkv, wqa, wqb, wk, wv, scale, tm):
    tp = pm.shape[0]
    row = lambda w: pl.BlockSpec((tm, w), lambda i: (i, 0))
    head = lambda w: pl.BlockSpec((MLA_HEADS, tm, w), lambda i: (0, i, 0))
    return pl.pallas_call(
        functools.partial(_mla_prep_kernel, scale=scale),
        grid=(tp // tm,),
        in_specs=[row(pm.shape[1]), row(128), row(128), _full(gq.shape), _full(gkv.shape),
                  _full(wqa.shape), _full(wqb.shape), _full(wk.shape), _full(wv.shape)],
        out_specs=[head(256), head(256), head(128)],
        out_shape=[jax.ShapeDtypeStruct((MLA_HEADS, tp, 256), BF16),
                   jax.ShapeDtypeStruct((MLA_HEADS, tp, 256), BF16),
                   jax.ShapeDtypeStruct((MLA_HEADS, tp, 128), BF16)],
        compiler_params=_params("parallel"),
        name="mla_prep",
    )(pm, cos, sin, gq, gkv, wqa, wqb, wk, wv)


ATT_Q = 256


def _attn_kernel(q_ref, k_ref, v_ref, o_ref, *, n_q):
    for qi in range(n_q):
        q0 = X0 + ATT_Q * qi
        kend = q0 + ATT_Q
        n = kend - K0
        q = q_ref[q0:q0 + ATT_Q, :]
        s = _dot_nt(q, k_ref[K0:kend, :])
        kk = lax.broadcasted_iota(jnp.int32, (1, n), 1)
        ck = jnp.where(kk < 128 - N_META, 1 << 20, kk >> 6)
        cq = ((ATT_Q * qi + lax.broadcasted_iota(jnp.int32, (ATT_Q, 1), 0)) >> 6) + 2
        s = jnp.where(ck <= cq, s, -1e30)
        e = jnp.exp(s - jnp.max(s, axis=-1, keepdims=True))
        l = jnp.sum(e, axis=-1, keepdims=True)
        o = _dot(e.astype(BF16), v_ref[K0:kend, :])
        o_ref[ATT_Q * qi:ATT_Q * (qi + 1), :] = o / l


def _attention(q, k, v, batch, lp, s_len):
    return pl.pallas_call(
        functools.partial(_attn_kernel, n_q=s_len // ATT_Q),
        grid=(batch, MLA_HEADS),
        in_specs=[pl.BlockSpec((None, lp, 256), lambda b, h: (h, b, 0)),
                  pl.BlockSpec((None, lp, 256), lambda b, h: (h, b, 0)),
                  pl.BlockSpec((None, lp, 128), lambda b, h: (h, b, 0))],
        out_specs=pl.BlockSpec((s_len, 128), lambda b, h: (b, h)),
        out_shape=jax.ShapeDtypeStruct((batch * s_len, MLA_HEADS * 128), F32),
        compiler_params=_params("parallel", "parallel"),
        name="mla_attn",
    )(q, k, v)


def _softplus(x):
    return jnp.maximum(x, 0.0) + jnp.log(1.0 + jnp.exp(-jnp.abs(x)))


def _sigmoid(x):
    return 1.0 / (1.0 + jnp.exp(-x))


def _rwkv_prep_kernel(p_ref, prev_ref, mu_ref, w0_ref, a0_ref, kk_ref, ka_ref, rk_ref,
                      wup_ref, aup_ref, gup_ref, bd_ref,
                      r_ref, lw_ref, k_ref, v_ref, kn_ref, b_ref, bonus_ref, g_ref, *, tm):
    p = p_ref[...]
    rows = lax.broadcasted_iota(jnp.int32, (tm, 1), 0)
    shifted = jnp.where(rows == 0, prev_ref[7:8, :], pltpu.roll(p, 1, 0))
    pm = p + mu_ref[...] * (shifted - p)
    r, k, v = pm[:, 0:512], pm[:, 512:1024], pm[:, 1024:1536]
    dd, dg = pm[:, 1536:1664], pm[:, 1664:1792]
    w = -_softplus(-(w0_ref[...] + _dot(jnp.tanh(dd).astype(BF16), wup_ref[...]))) - 0.5
    log_decay = -jnp.exp(w)
    a = _sigmoid(a0_ref[...] + _dot(dd.astype(BF16), aup_ref[...]))
    g_ref[...] = _dot(_sigmoid(dg).astype(BF16), gup_ref[...])
    bd = bd_ref[...]
    kk = k * kk_ref[...]
    norm = jnp.sqrt(_dot(kk * kk, bd, HIGHEST))
    kk = kk / jnp.maximum(norm, 1e-12)
    k2 = k * (1.0 + (a - 1.0) * ka_ref[...])
    real = (pl.program_id(1) * tm + rows) >= META0
    r_ref[...] = r
    lw_ref[...] = jnp.where(real, log_decay, 0.0)
    k_ref[...] = k2
    v_ref[...] = v
    kn_ref[...] = kk
    b_ref[...] = kk * a
    bonus_ref[...] = _dot(r * k2 * rk_ref[...], bd, HIGHEST) * v


def _rwkv_prep(pr, mu, w0, a0, k_k, k_a, r_k, wup, aup, gup, bd, batch, lp, tm):
    tp, width = pr.shape
    nl = lp // tm
    vec = lambda a: _full(a.shape)
    out = pl.BlockSpec((tm, 512), lambda b, j: (b * nl + j, 0))
    return pl.pallas_call(
        functools.partial(_rwkv_prep_kernel, tm=tm),
        grid=(batch, nl),
        in_specs=[pl.BlockSpec((tm, width), lambda b, j: (b * nl + j, 0)),
                  pl.BlockSpec((8, width), lambda b, j: (jnp.maximum((b * nl + j) * (tm // 8) - 1, 0), 0)),
                  vec(mu), vec(w0), vec(a0), vec(k_k), vec(k_a), vec(r_k),
                  vec(wup), vec(aup), vec(gup), vec(bd)],
        out_specs=[out] * 8,
        out_shape=[jax.ShapeDtypeStruct((tp, 512), F32)] * 8,
        compiler_params=_params("parallel", "parallel"),
        name="rwkv_prep",
    )(pr, pr, mu, w0, a0, k_k, k_a, r_k, wup, aup, gup, bd)


SCAN_PRECISION = HIGHEST


def _rwkv_scan_kernel(r_ref, lw_ref, k_ref, v_ref, kn_ref, b_ref, o_ref, z_ref):
    c = CHUNK
    hp = SCAN_PRECISION

    @pl.when(pl.program_id(2) == 0)
    def _():
        z_ref[...] = jnp.zeros_like(z_ref)

    ri = lax.broadcasted_iota(jnp.int32, (256, 256), 0)
    ci = lax.broadcasted_iota(jnp.int32, (256, 256), 1)
    same_head = (ri >> 6) == (ci >> 6)
    strict = (ri & 63) > (ci & 63)
    incl = (ri & 63) >= (ci & 63)
    eye = ri == ci
    tri = (lax.broadcasted_iota(jnp.int32, (c, c), 0) >= lax.broadcasted_iota(jnp.int32, (c, c), 1)).astype(F32)

    def stack(x):
        return jnp.where(same_head, jnp.concatenate([x, x, x, x], axis=0), 0.0)

    lw = lw_ref[...]
    cum = _dot(tri, lw, HIGHEST)
    cum_prev = cum - lw
    mid = cum[c // 2 - 1:c // 2, :]
    tot = cum[c - 1:c, :]
    a = -kn_ref[...]
    b = b_ref[...]
    k = k_ref[...]
    r = r_ref[...]
    dn = jnp.exp(mid - cum)
    de = jnp.exp(tot - cum)
    a_t = stack(a * jnp.exp(cum_prev - mid))
    a_0 = stack(a * jnp.exp(cum_prev))
    b_t = stack(b * dn)
    k_t = stack(k * dn)
    r_t = stack(r * jnp.exp(cum - mid))
    r_0 = stack(r * jnp.exp(cum))
    b_e = stack(b * de)
    k_e = stack(k * de)
    vs = stack(v_ref[...])

    l_ab = jnp.where(strict, _dot_nt(a_t, b_t, hp), 0.0)
    l_ak = jnp.where(strict, _dot_nt(a_t, k_t, hp), 0.0)
    m_rb = jnp.where(incl, _dot_nt(r_t, b_t, hp), 0.0)
    m_rk = jnp.where(incl, _dot_nt(r_t, k_t, hp), 0.0)

    t_inv = jnp.where(eye, 1.0, 0.0) + l_ab
    power = l_ab
    for _ in range(int(math.log2(c)) - 1):
        power = _dot(power, power, hp)
        t_inv = t_inv + _dot(power, t_inv, hp)

    p = _dot(t_inv, a_0, hp)
    q = _dot(t_inv, l_ak, hp)
    z = z_ref[...]
    r_hat = r_0 + _dot(m_rb, p, hp)
    m_o = _dot(m_rb, q, hp) + m_rk
    o_rows = _dot(r_hat, z, hp) + _dot(m_o, vs, hp)
    o_ref[...] = o_rows[0:c] + o_rows[c:2 * c] + o_rows[2 * c:3 * c] + o_rows[3 * c:4 * c]

    b_et = b_e.T
    g = jnp.where(eye, jnp.exp(tot), 0.0) + _dot(b_et, p, hp)
    m_h = _dot(b_et, q, hp) + k_e.T
    z_ref[...] = _dot(g, z, hp) + _dot(m_h, vs, hp)


def _rwkv_scan(r, lw, k, v, kn, b, batch, lp, s_len):
    nc = s_len // CHUNK + 1
    rows_per_b = lp // CHUNK
    nx = s_len // CHUNK
    blk = pl.BlockSpec((CHUNK, 256), lambda bi, q, ci: (bi * rows_per_b + SCAN_C0 + ci, q))
    return pl.pallas_call(
        _rwkv_scan_kernel,
        grid=(batch, 2, nc),
        in_specs=[blk] * 6,
        out_specs=pl.BlockSpec((CHUNK, 256), lambda bi, q, ci: (bi * nx + jnp.maximum(ci - 1, 0), q)),
        out_shape=jax.ShapeDtypeStruct((batch * s_len, 512), F32),
        scratch_shapes=[pltpu.VMEM((256, 256), F32)],
        compiler_params=_params("parallel", "parallel", "arbitrary"),
        name="rwkv_scan",
    )(r, lw, k, v, kn, b)


def _outproj_kernel(x_ref, om_ref, orw_ref, bonus_ref, g_ref, og_ref, lnw_ref, lnb_ref, bd_ref,
                    wout_ref, gffn_ref, wq_ref, h2_ref, t_ref, q_ref):
    y_mla = _rms(om_ref[...], og_ref[...])
    o = orw_ref[...]
    bd = bd_ref[...] * (1.0 / RWKV_HEAD)
    mean = _dot(o, bd, HIGHEST)
    cen = o - mean
    var = _dot(cen * cen, bd, HIGHEST)
    y_rwkv = (cen * lax.rsqrt(var + GN_EPS) * lnw_ref[...] + lnb_ref[...] + bonus_ref[...]) * g_ref[...]
    y = jnp.concatenate([y_mla, y_rwkv], axis=-1).astype(BF16)
    h2 = x_ref[...] + _dot(y, wout_ref[...])
    h2_ref[...] = h2
    t = _rms(h2, gffn_ref[...]).astype(BF16)
    t_ref[...] = t
    q_ref[...] = _dot(t, wq_ref[...])


def _outproj(x, o_mla, o_rwkv, bonus, g, og, lnw, lnb, bd, wout, gffn, wq, batch, lp, s_len, tm):
    t_all, d = x.shape
    ns = s_len // tm
    npad = lp // tm
    xrow = lambda w: pl.BlockSpec((tm, w), lambda b, j: (b * ns + j, 0))
    prow = lambda w: pl.BlockSpec((tm, w), lambda b, j: (b * npad + X0 // tm + j, 0))
    vec = lambda a: _full(a.shape)
    return pl.pallas_call(
        _outproj_kernel,
        grid=(batch, ns),
        in_specs=[xrow(d), xrow(512), xrow(512), prow(512), prow(512), vec(og), vec(lnw), vec(lnb), vec(bd),
                  vec(wout), vec(gffn), vec(wq)],
        out_specs=[xrow(d), xrow(d), xrow(wq.shape[1])],
        out_shape=[jax.ShapeDtypeStruct((t_all, d), F32), jax.ShapeDtypeStruct((t_all, d), BF16),
                   jax.ShapeDtypeStruct((t_all, wq.shape[1]), F32)],
        compiler_params=_params("parallel", "parallel"),
        name="outproj",
    )(x, o_mla, o_rwkv, bonus, g, og, lnw, lnb, bd, wout, gffn, wq)


_CANDS = [(i, j) for i in range(PEER_TOPK) for j in range(PEER_TOPK) if (i + 1) * (j + 1) <= PEER_TOPK]


def _top_distinct(x, weight, n):
    out = []
    for _ in range(n):
        m = jnp.max(x, axis=0, keepdims=True)
        eq = x == m
        cnt = jnp.sum(jnp.where(eq, weight, 0.0), axis=0, keepdims=True)
        x = jnp.where(eq, -jnp.inf, x)
        out.append((m, cnt))
    return out


def _route_kernel(q_ref, keys_ref, s1_ref, s2_ref, e1_ref, e2_ref, tau_ref, *, heads):
    qb = q_ref[...].astype(BF16)
    neg = jnp.full((6, qb.shape[0]), -jnp.inf, F32)
    for h in range(heads):
        s = [_dot_nt(keys_ref[2 * h + c], qb[:, 128 * (2 * h + c):128 * (2 * h + c + 1)]) for c in range(2)]
        top = [_top_distinct(s[c], 1.0, PEER_TOPK) for c in range(2)]
        cv = jnp.concatenate([top[0][i][0] + top[1][j][0] for i, j in _CANDS] + [neg], axis=0)
        cw = jnp.concatenate([top[0][i][1] * top[1][j][1] for i, j in _CANDS] + [jnp.zeros_like(neg)], axis=0)
        tau = jnp.full_like(top[0][0][0], -jnp.inf)
        seen = jnp.zeros_like(tau)
        for m, cnt in _top_distinct(cv, cw, PEER_TOPK):
            seen = seen + cnt
            tau = jnp.maximum(tau, jnp.where(seen >= PEER_TOPK, m, -jnp.inf))
        m1, m2 = top[0][0][0], top[1][0][0]
        z = jnp.sum(jnp.where(cv >= tau, cw * jnp.exp(cv - (m1 + m2)), 0.0), axis=0, keepdims=True)
        s1_ref[h] = s[0]
        s2_ref[h] = s[1]
        e1_ref[h] = jnp.exp(s[0] - m1) / z
        e2_ref[h] = jnp.exp(s[1] - m2)
        tau_ref[h:h + 1, :] = tau


def _route(q, keys, heads, tm):
    t_all = q.shape[0]
    big = pl.BlockSpec((heads, 128, tm), lambda i: (0, 0, i))
    return pl.pallas_call(
        functools.partial(_route_kernel, heads=heads),
        grid=(t_all // tm,),
        in_specs=[pl.BlockSpec((tm, q.shape[1]), lambda i: (i, 0)), _full(keys.shape)],
        out_specs=[big] * 4 + [pl.BlockSpec((heads, tm), lambda i: (0, i))],
        out_shape=[jax.ShapeDtypeStruct((heads, 128, t_all), F32)] * 4
                  + [jax.ShapeDtypeStruct((heads, t_all), F32)],
        compiler_params=_params("parallel"),
        name="peer_route",
    )(q, keys)


def _peer_kernel(t_ref, u_ref, vt_ref, s1_ref, s2_ref, e1_ref, e2_ref, tau_ref, h2_ref, gf_ref,
                 y_ref, acc_ref, *, heads, te):
    j = pl.program_id(1)

    @pl.when(j == 0)
    def _():
        acc_ref[...] = jnp.zeros_like(acc_ref)

    hp = _dot_nt(u_ref[...], t_ref[...])
    act = 0.5 * hp * (1.0 + lax.erf(hp * (1.0 / math.sqrt(2.0))))
    gates = []
    for ai in range(te // 128):
        a = j * (te // 128) + ai
        gate = jnp.zeros((128, hp.shape[1]), F32)
        for h in range(heads):
            total = s2_ref[h] + s1_ref[h, pl.ds(a, 1), :]
            gate = gate + jnp.where(total >= tau_ref[h:h + 1, :], e2_ref[h], 0.0) * e1_ref[h, pl.ds(a, 1), :]
        gates.append(gate)
    act = (act * jnp.concatenate(gates, axis=0)).astype(BF16)
    acc_ref[...] += _dot(vt_ref[...], act)

    @pl.when(j == pl.num_programs(1) - 1)
    def _():
        y_ref[...] = _rms(h2_ref[...] + acc_ref[...].T, gf_ref[...])


def _peer(t, u, vt, s1, s2, e1, e2, tau, h2, gf, heads, tm, te):
    t_all, d = t.shape
    n_exp = u.shape[0]
    tok = pl.BlockSpec((heads, 128, tm), lambda i, j: (0, 0, i))
    return pl.pallas_call(
        functools.partial(_peer_kernel, heads=heads, te=te),
        grid=(t_all // tm, n_exp // te),
        in_specs=[pl.BlockSpec((tm, d), lambda i, j: (i, 0)),
                  pl.BlockSpec((te, d), lambda i, j: (j, 0)),
                  pl.BlockSpec((d, te), lambda i, j: (0, j)),
                  tok, tok, tok, tok,
                  pl.BlockSpec((heads, tm), lambda i, j: (0, i)),
                  pl.BlockSpec((tm, d), lambda i, j: (i, 0)),
                  _full(gf.shape)],
        out_specs=pl.BlockSpec((tm, d), lambda i, j: (i, 0)),
        out_shape=jax.ShapeDtypeStruct((t_all, d), F32),
        scratch_shapes=[pltpu.VMEM((d, tm), F32)],
        compiler_params=_params("parallel", "arbitrary"),
        name="peer_experts",
    )(t, u, vt, s1, s2, e1, e2, tau, h2, gf)


def _tile(n, cap):
    t = cap
    while n % t:
        t //= 2
    return t


def kernel(x, meta_tokens, norm_mix_g, w_in, mla_q_norm_g, mla_w_uq, mla_kv_norm_g, mla_w_ukv, mla_out_g, rwkv_mu, rwkv_w0, rwkv_w_up, rwkv_a0, rwkv_a_up, rwkv_g_up, rwkv_k_k, rwkv_k_a, rwkv_r_k, rwkv_ln_w, rwkv_ln_b, w_out, norm_ffn_g, peer_w_q, peer_sub_keys, peer_u, peer_v, norm_final_g):
    batch, s_len, d = x.shape
    assert w_in.shape[0] == 1, "one layer"
    assert s_len % ATT_Q == 0 and d == 1024
    lp = X0 + s_len
    tp = batch * lp
    row = lambda a: a.reshape(1, -1).astype(F32)

    h = jnp.concatenate([jnp.zeros((batch, META0, d), F32),
                         jnp.broadcast_to(meta_tokens[None].astype(F32), (batch, N_META, d)), x], axis=1)
    h = h.reshape(tp, d)

    wi = w_in[0]
    kr = wi[:, 384:448]
    z64 = jnp.zeros((d, 64), F32)
    w_ext = jnp.concatenate([wi[:, 0:384], kr, z64, kr[:, 32:], kr[:, :32], z64, wi[:, 448:]], axis=1).astype(BF16)
    n_mla = 640
    pm, pr = _inproj(h, row(norm_mix_g[0]), w_ext, n_mla, _tile(tp, 512))

    pos = jnp.maximum(jnp.arange(lp) - META0, 0).astype(F32)
    inv = ROPE_THETA ** (-jnp.arange(32, dtype=F32) / 32)
    ang = pos[:, None] * inv[None, :]
    zl = jnp.zeros((lp, 64), F32)
    cos = jnp.tile(jnp.concatenate([jnp.cos(ang), jnp.cos(ang), zl], axis=1), (batch, 1))
    sin = jnp.tile(jnp.concatenate([-jnp.sin(ang), jnp.sin(ang), zl], axis=1), (batch, 1))

    wuq = mla_w_uq[0].reshape(256, MLA_HEADS, 192)
    zq = jnp.zeros((256, MLA_HEADS, 64), F32)
    wqa = jnp.concatenate([wuq, zq], axis=2).reshape(256, MLA_HEADS * 256).astype(BF16)
    wqb = jnp.concatenate([wuq[:, :, 160:192], wuq[:, :, 128:160], zq], axis=2).reshape(256, MLA_HEADS * 128).astype(BF16)
    wukv = mla_w_ukv[0].reshape(128, MLA_HEADS, 256)
    wk = wukv[:, :, :128].reshape(128, MLA_HEADS * 128).astype(BF16)
    wv = wukv[:, :, 128:].reshape(128, MLA_HEADS * 128).astype(BF16)
    q, k, v = _mla_prep(pm, cos, sin, row(mla_q_norm_g[0]), row(mla_kv_norm_g[0]), wqa, wqb, wk, wv,
                        192 ** -0.5, _tile(tp, 512))
    o_mla = _attention(q, k, v, batch, lp, s_len)

    z_up = jnp.zeros((64, 512), F32)
    wup = jnp.concatenate([rwkv_w_up[0], z_up], axis=0).astype(BF16)
    aup = jnp.concatenate([z_up, rwkv_a_up[0]], axis=0).astype(BF16)
    head_id = jnp.arange(512) // RWKV_HEAD
    bd = (head_id[:, None] == head_id[None, :]).astype(F32)
    r, lw, k2, vv, kn, bv, bonus, g = _rwkv_prep(
        pr, row(rwkv_mu[0]), row(rwkv_w0[0]), row(rwkv_a0[0]), row(rwkv_k_k[0]), row(rwkv_k_a[0]),
        row(rwkv_r_k[0]), wup, aup, rwkv_g_up[0].astype(BF16), bd, batch, lp, 256)
    o_rwkv = _rwkv_scan(r, lw, k2, vv, kn, bv, batch, lp, s_len)

    heads = peer_sub_keys.shape[1]
    h2, t, pq = _outproj(x.reshape(batch * s_len, d), o_mla, o_rwkv, bonus, g, row(mla_out_g[0]),
                         row(rwkv_ln_w[0]), row(rwkv_ln_b[0]), bd, w_out[0].astype(BF16), row(norm_ffn_g[0]),
                         peer_w_q[0].astype(BF16), batch, lp, s_len, _tile(s_len, 512))

    t_all = batch * s_len
    keys = peer_sub_keys[0].reshape(heads * 2, 128, 128).astype(BF16)
    tm = _tile(t_all, 512)
    s1, s2, e1, e2, tau = _route(pq, keys, heads, tm)
    y = _peer(t, peer_u[0].astype(BF16), peer_v[0].T.astype(BF16), s1, s2, e1, e2, tau, h2,
              row(norm_final_g), heads, tm, 512)
    return y.reshape(batch, s_len, d)
```

```python
import functools
import math

import jax
import jax.numpy as jnp
from jax import lax
from jax.experimental import pallas as pl
from jax.experimental.pallas import tpu as pltpu

F32 = jnp.float32
BF16 = jnp.bfloat16
HIGHEST = lax.Precision.HIGHEST

RMS_EPS = 1e-6
GN_EPS = 64e-5
ROPE_THETA = 10000.0
N_META = 16
CHUNK = 64
X0 = 512
META0 = X0 - N_META
K0 = X0 - 128
SCAN_C0 = (X0 - CHUNK) // CHUNK
MLA_HEADS = 4
RWKV_HEAD = 64
PEER_TOPK = 16
VMEM_LIMIT = 56 * 1024 * 1024


def _dot(a, b, precision=None):
    return jnp.dot(a, b, preferred_element_type=F32, precision=precision)


def _dot_nt(a, b, precision=None):
    return lax.dot_general(a, b, (((1,), (1,)), ((), ())),
                           preferred_element_type=F32, precision=precision)


def _rms(x, g):
    return x * lax.rsqrt(jnp.mean(x * x, axis=-1, keepdims=True) + RMS_EPS) * g


def _params(*sem):
    return pltpu.CompilerParams(dimension_semantics=sem, vmem_limit_bytes=VMEM_LIMIT)


def _full(shape):
    n = len(shape)
    return pl.BlockSpec(shape, lambda *_: (0,) * n)


def _inproj_kernel(h_ref, g_ref, w_ref, pm_ref, pr_ref, *, n_mla):
    n = _rms(h_ref[...], g_ref[...]).astype(BF16)
    p = _dot(n, w_ref[...])
    pm_ref[...] = p[:, :n_mla]
    pr_ref[...] = p[:, n_mla:]


def _inproj(h, g, w, n_mla, tm):
    tp, d = h.shape
    n_all = w.shape[1]
    return pl.pallas_call(
        functools.partial(_inproj_kernel, n_mla=n_mla),
        grid=(tp // tm,),
        in_specs=[pl.BlockSpec((tm, d), lambda i: (i, 0)), _full((1, d)), _full(w.shape)],
        out_specs=[pl.BlockSpec((tm, n_mla), lambda i: (i, 0)),
                   pl.BlockSpec((tm, n_all - n_mla), lambda i: (i, 0))],
        out_shape=[jax.ShapeDtypeStruct((tp, n_mla), F32),
                   jax.ShapeDtypeStruct((tp, n_all - n_mla), F32)],
        compiler_params=_params("parallel"),
        name="inproj",
    )(h, g, w)


def _mla_prep_kernel(pm_ref, cos_ref, sin_ref, gq_ref, gkv_ref, wqa_ref, wqb_ref, wk_ref, wv_ref,
                     q_ref, k_ref, v_ref, *, scale):
    pm = pm_ref[...]
    nq = _rms(pm[:, 0:256], gq_ref[...]).astype(BF16)
    nkv = _rms(pm[:, 256:384], gkv_ref[...]).astype(BF16)
    cos, sin = cos_ref[...], sin_ref[...]
    qa = _dot(nq, wqa_ref[...])
    qb = _dot(nq, wqb_ref[...])
    kn = _dot(nkv, wk_ref[...])
    vv = _dot(nkv, wv_ref[...])
    kr = pm[:, 384:512] * cos + pm[:, 512:640] * sin
    for h in range(MLA_HEADS):
        qn = qa[:, 256 * h:256 * h + 128]
        qr = qa[:, 256 * h + 128:256 * h + 256] * cos + qb[:, 128 * h:128 * h + 128] * sin
        q_ref[h] = (jnp.concatenate([qn, qr], axis=-1) * scale).astype(BF16)
        k_ref[h] = jnp.concatenate([kn[:, 128 * h:128 * h + 128], kr], axis=-1).astype(BF16)
        v_ref[h] = vv[:, 128 * h:128 * h + 128].astype(BF16)


def _mla_prep(pm, cos, sin, gq, gkv, wqa, wqb, wk, wv, scale, tm):
    tp = pm.shape[0]
    row = lambda w: pl.BlockSpec((tm, w), lambda i: (i, 0))
    head = lambda w: pl.BlockSpec((MLA_HEADS, tm, w), lambda i: (0, i, 0))
    return pl.pallas_call(
        functools.partial(_mla_prep_kernel, scale=scale),
        grid=(tp // tm,),
        in_specs=[row(pm.shape[1]), row(128), row(128), _full(gq.shape), _full(gkv.shape),
                  _full(wqa.shape), _full(wqb.shape), _full(wk.shape), _full(wv.shape)],
        out_specs=[head(256), head(256), head(128)],
        out_shape=[jax.ShapeDtypeStruct((MLA_HEADS, tp, 256), BF16),
                   jax.ShapeDtypeStruct((MLA_HEADS, tp, 256), BF16),
                   jax.ShapeDtypeStruct((MLA_HEADS, tp, 128), BF16)],
        compiler_params=_params("parallel"),
        name="mla_prep",
    )(pm, cos, sin, gq, gkv, wqa, wqb, wk, wv)


ATT_Q = 256


def _attn_kernel(q_ref, k_ref, v_ref, o_ref, *, n_q):
    for qi in range(n_q):
        q0 = X0 + ATT_Q * qi
        kend = q0 + ATT_Q
        n = kend - K0
        q = q_ref[q0:q0 + ATT_Q, :]
        s = _dot_nt(q, k_ref[K0:kend, :])
        kk = lax.broadcasted_iota(jnp.int32, (1, n), 1)
        ck = jnp.where(kk < 128 - N_META, 1 << 20, kk >> 6)
        cq = ((ATT_Q * qi + lax.broadcasted_iota(jnp.int32, (ATT_Q, 1), 0)) >> 6) + 2
        s = jnp.where(ck <= cq, s, -1e30)
        e = jnp.exp(s - jnp.max(s, axis=-1, keepdims=True))
        l = jnp.sum(e, axis=-1, keepdims=True)
        o = _dot(e.astype(BF16), v_ref[K0:kend, :])
        o_ref[ATT_Q * qi:ATT_Q * (qi + 1), :] = o / l


def _attention(q, k, v, batch, lp, s_len):
    return pl.pallas_call(
        functools.partial(_attn_kernel, n_q=s_len // ATT_Q),
        grid=(batch, MLA_HEADS),
        in_specs=[pl.BlockSpec((None, lp, 256), lambda b, h: (h, b, 0)),
                  pl.BlockSpec((None, lp, 256), lambda b, h: (h, b, 0)),
                  pl.BlockSpec((None, lp, 128), lambda b, h: (h, b, 0))],
        out_specs=pl.BlockSpec((s_len, 128), lambda b, h: (b, h)),
        out_shape=jax.ShapeDtypeStruct((batch * s_len, MLA_HEADS * 128), F32),
        compiler_params=_params("parallel", "parallel"),
        name="mla_attn",
    )(q, k, v)


def _softplus(x):
    return jnp.maximum(x, 0.0) + jnp.log(1.0 + jnp.exp(-jnp.abs(x)))


def _sigmoid(x):
    return 1.0 / (1.0 + jnp.exp(-x))


def _rwkv_prep_kernel(p_ref, prev_ref, mu_ref, w0_ref, a0_ref, kk_ref, ka_ref, rk_ref,
                      wup_ref, aup_ref, gup_ref, bd_ref,
                      r_ref, lw_ref, k_ref, v_ref, kn_ref, b_ref, bonus_ref, g_ref, *, tm):
    p = p_ref[...]
    rows = lax.broadcasted_iota(jnp.int32, (tm, 1), 0)
    shifted = jnp.where(rows == 0, prev_ref[7:8, :], pltpu.roll(p, 1, 0))
    pm = p + mu_ref[...] * (shifted - p)
    r, k, v = pm[:, 0:512], pm[:, 512:1024], pm[:, 1024:1536]
    dd, dg = pm[:, 1536:1664], pm[:, 1664:1792]
    w = -_softplus(-(w0_ref[...] + _dot(jnp.tanh(dd).astype(BF16), wup_ref[...]))) - 0.5
    log_decay = -jnp.exp(w)
    a = _sigmoid(a0_ref[...] + _dot(dd.astype(BF16), aup_ref[...]))
    g_ref[...] = _dot(_sigmoid(dg).astype(BF16), gup_ref[...])
    bd = bd_ref[...]
    kk = k * kk_ref[...]
    norm = jnp.sqrt(_dot(kk * kk, bd, HIGHEST))
    kk = kk / jnp.maximum(norm, 1e-12)
    k2 = k * (1.0 + (a - 1.0) * ka_ref[...])
    real = (pl.program_id(1) * tm + rows) >= META0
    r_ref[...] = r
    lw_ref[...] = jnp.where(real, log_decay, 0.0)
    k_ref[...] = k2
    v_ref[...] = v
    kn_ref[...] = kk
    b_ref[...] = kk * a
    bonus_ref[...] = _dot(r * k2 * rk_ref[...], bd, HIGHEST) * v


def _rwkv_prep(pr, mu, w0, a0, k_k, k_a, r_k, wup, aup, gup, bd, batch, lp, tm):
    tp, width = pr.shape
    nl = lp // tm
    vec = lambda a: _full(a.shape)
    out = pl.BlockSpec((tm, 512), lambda b, j: (b * nl + j, 0))
    return pl.pallas_call(
        functools.partial(_rwkv_prep_kernel, tm=tm),
        grid=(batch, nl),
        in_specs=[pl.BlockSpec((tm, width), lambda b, j: (b * nl + j, 0)),
                  pl.BlockSpec((8, width), lambda b, j: (jnp.maximum((b * nl + j) * (tm // 8) - 1, 0), 0)),
                  vec(mu), vec(w0), vec(a0), vec(k_k), vec(k_a), vec(r_k),
                  vec(wup), vec(aup), vec(gup), vec(bd)],
        out_specs=[out] * 8,
        out_shape=[jax.ShapeDtypeStruct((tp, 512), F32)] * 8,
        compiler_params=_params("parallel", "parallel"),
        name="rwkv_prep",
    )(pr, pr, mu, w0, a0, k_k, k_a, r_k, wup, aup, gup, bd)


def _bdot(a, b):
    return _dot(a.astype(BF16), b.astype(BF16))


def _rwkv_scan_kernel(r_ref, lw_ref, k_ref, v_ref, kn_ref, b_ref, o_ref, z_ref):
    c = CHUNK
    w = 4 * c

    @pl.when(pl.program_id(2) == 0)
    def _():
        z_ref[...] = jnp.zeros_like(z_ref)

    ri = lax.broadcasted_iota(jnp.int32, (w, w), 0)
    ci = lax.broadcasted_iota(jnp.int32, (w, w), 1)
    same_head = (ri >> 6) == (ci >> 6)
    strict = (ri & 63) > (ci & 63)
    incl = (ri & 63) >= (ci & 63)
    eye = ri == ci
    tri = (lax.broadcasted_iota(jnp.int32, (c, c), 0) >= lax.broadcasted_iota(jnp.int32, (c, c), 1)).astype(F32)

    def stack(x):
        return jnp.where(same_head, jnp.concatenate([x, x, x, x], axis=0), 0.0)

    lw = lw_ref[...]
    cum = _dot(tri, lw, HIGHEST)
    cum_prev = cum - lw
    mid = cum[c // 2 - 1:c // 2, :]
    tot = cum[c - 1:c, :]
    a = -kn_ref[...]
    b = b_ref[...]
    k = k_ref[...]
    r = r_ref[...]
    dn = jnp.exp(mid - cum)
    de = jnp.exp(tot - cum)
    a_t = stack(a * jnp.exp(cum_prev - mid))
    a_0 = stack(a * jnp.exp(cum_prev))
    b_t = stack(b * dn)
    k_t = stack(k * dn)
    r_t = stack(r * jnp.exp(cum - mid))
    r_0 = stack(r * jnp.exp(cum))
    b_e = stack(b * de)
    k_e = stack(k * de)
    vs = stack(v_ref[...])

    inter = _dot_nt(jnp.concatenate([a_t, r_t], axis=0).astype(BF16),
                    jnp.concatenate([b_t, k_t], axis=0).astype(BF16))
    l_ab = jnp.where(strict, inter[:w, :w], 0.0)
    l_ak = jnp.where(strict, inter[:w, w:], 0.0)
    m_rb = jnp.where(incl, inter[w:, :w], 0.0)
    m_rk = jnp.where(incl, inter[w:, w:], 0.0)

    t_inv = jnp.where(eye, 1.0, 0.0) + l_ab
    power = _bdot(l_ab, l_ab)
    for _ in range(int(math.log2(c)) - 2):
        both = _bdot(power, jnp.concatenate([power, t_inv], axis=1))
        power, t_inv = both[:, :w], t_inv + both[:, w:]
    t_inv = t_inv + _bdot(power, t_inv)

    pq = _bdot(t_inv, jnp.concatenate([a_0, l_ak], axis=1))
    mix = _bdot(jnp.concatenate([m_rb, b_e.T], axis=0), pq)
    r_hat = r_0 + mix[:w, :w]
    m_o = mix[:w, w:] + m_rk
    g = jnp.where(eye, jnp.exp(tot), 0.0) + mix[w:, :w]
    m_h = mix[w:, w:] + k_e.T
    lhs = jnp.concatenate([jnp.concatenate([r_hat, m_o], axis=1), jnp.concatenate([g, m_h], axis=1)], axis=0)
    new = _bdot(lhs, jnp.concatenate([z_ref[...], vs], axis=0))
    o_rows = new[:w]
    o_ref[...] = o_rows[0:c] + o_rows[c:2 * c] + o_rows[2 * c:3 * c] + o_rows[3 * c:4 * c]
    z_ref[...] = new[w:]


def _rwkv_scan(r, lw, k, v, kn, b, batch, lp, s_len):
    nc = s_len // CHUNK + 1
    rows_per_b = lp // CHUNK
    nx = s_len // CHUNK
    blk = pl.BlockSpec((CHUNK, 256), lambda bi, q, ci: (bi * rows_per_b + SCAN_C0 + ci, q))
    return pl.pallas_call(
        _rwkv_scan_kernel,
        grid=(batch, 2, nc),
        in_specs=[blk] * 6,
        out_specs=pl.BlockSpec((CHUNK, 256), lambda bi, q, ci: (bi * nx + jnp.maximum(ci - 1, 0), q)),
        out_shape=jax.ShapeDtypeStruct((batch * s_len, 512), F32),
        scratch_shapes=[pltpu.VMEM((256, 256), F32)],
        compiler_params=_params("parallel", "parallel", "arbitrary"),
        name="rwkv_scan",
    )(r, lw, k, v, kn, b)


def _outproj_kernel(x_ref, om_ref, orw_ref, bonus_ref, g_ref, og_ref, lnw_ref, lnb_ref, bd_ref,
                    wout_ref, gffn_ref, wq_ref, h2_ref, t_ref, q_ref):
    y_mla = _rms(om_ref[...], og_ref[...])
    o = orw_ref[...]
    bd = bd_ref[...] * (1.0 / RWKV_HEAD)
    mean = _dot(o, bd, HIGHEST)
    cen = o - mean
    var = _dot(cen * cen, bd, HIGHEST)
    y_rwkv = (cen * lax.rsqrt(var + GN_EPS) * lnw_ref[...] + lnb_ref[...] + bonus_ref[...]) * g_ref[...]
    y = jnp.concatenate([y_mla, y_rwkv], axis=-1).astype(BF16)
    h2 = x_ref[...] + _dot(y, wout_ref[...])
    h2_ref[...] = h2
    t = _rms(h2, gffn_ref[...]).astype(BF16)
    t_ref[...] = t
    q_ref[...] = _dot(t, wq_ref[...])


def _outproj(x, o_mla, o_rwkv, bonus, g, og, lnw, lnb, bd, wout, gffn, wq, batch, lp, s_len, tm):
    t_all, d = x.shape
    ns = s_len // tm
    npad = lp // tm
    xrow = lambda w: pl.BlockSpec((tm, w), lambda b, j: (b * ns + j, 0))
    prow = lambda w: pl.BlockSpec((tm, w), lambda b, j: (b * npad + X0 // tm + j, 0))
    vec = lambda a: _full(a.shape)
    return pl.pallas_call(
        _outproj_kernel,
        grid=(batch, ns),
        in_specs=[xrow(d), xrow(512), xrow(512), prow(512), prow(512), vec(og), vec(lnw), vec(lnb), vec(bd),
                  vec(wout), vec(gffn), vec(wq)],
        out_specs=[xrow(d), xrow(d), xrow(wq.shape[1])],
        out_shape=[jax.ShapeDtypeStruct((t_all, d), F32), jax.ShapeDtypeStruct((t_all, d), BF16),
                   jax.ShapeDtypeStruct((t_all, wq.shape[1]), F32)],
        compiler_params=_params("parallel", "parallel"),
        name="outproj",
    )(x, o_mla, o_rwkv, bonus, g, og, lnw, lnb, bd, wout, gffn, wq)


_CANDS = [(i, j) for i in range(PEER_TOPK) for j in range(PEER_TOPK) if (i + 1) * (j + 1) <= PEER_TOPK]


def _top_distinct(x, weight, n):
    out = []
    for _ in range(n):
        m = jnp.max(x, axis=0, keepdims=True)
        eq = x == m
        cnt = jnp.sum(jnp.where(eq, weight, 0.0), axis=0, keepdims=True)
        x = jnp.where(eq, -jnp.inf, x)
        out.append((m, cnt))
    return out


def _route_kernel(q_ref, keys_ref, s1_ref, s2_ref, e1_ref, e2_ref, tau_ref, *, heads, tm):
    qb = q_ref[...].astype(BF16)
    neg = jnp.full((6, qb.shape[0]), -jnp.inf, F32)
    for h in range(heads):
        s = [_dot_nt(keys_ref[2 * h + c], qb[:, 128 * (2 * h + c):128 * (2 * h + c + 1)]) for c in range(2)]
        top = [_top_distinct(s[c], 1.0, PEER_TOPK) for c in range(2)]
        cv = jnp.concatenate([top[0][i][0] + top[1][j][0] for i, j in _CANDS] + [neg], axis=0)
        cw = jnp.concatenate([top[0][i][1] * top[1][j][1] for i, j in _CANDS] + [jnp.zeros_like(neg)], axis=0)
        tau = jnp.full_like(top[0][0][0], -jnp.inf)
        seen = jnp.zeros_like(tau)
        for m, cnt in _top_distinct(cv, cw, PEER_TOPK):
            seen = seen + cnt
            tau = jnp.maximum(tau, jnp.where(seen >= PEER_TOPK, m, -jnp.inf))
        m1, m2 = top[0][0][0], top[1][0][0]
        z = jnp.sum(jnp.where(cv >= tau, cw * jnp.exp(cv - (m1 + m2)), 0.0), axis=0, keepdims=True)
        e1 = jnp.exp(s[0] - m1) / z
        e2 = jnp.exp(s[1] - m2)
        for lb in range(tm // 128):
            lanes = slice(128 * lb, 128 * lb + 128)
            s1_ref[lb, h] = s[0][:, lanes]
            s2_ref[lb, h] = s[1][:, lanes]
            e1_ref[lb, h] = e1[:, lanes]
            e2_ref[lb, h] = e2[:, lanes]
            tau_ref[lb, h:h + 1, :] = tau[:, lanes]


def _route(q, keys, heads, tm):
    t_all = q.shape[0]
    big = pl.BlockSpec((tm // 128, heads, 128, 128), lambda i: (i, 0, 0, 0))
    return pl.pallas_call(
        functools.partial(_route_kernel, heads=heads, tm=tm),
        grid=(t_all // tm,),
        in_specs=[pl.BlockSpec((tm, q.shape[1]), lambda i: (i, 0)), _full(keys.shape)],
        out_specs=[big] * 4 + [pl.BlockSpec((tm // 128, heads, 128), lambda i: (i, 0, 0))],
        out_shape=[jax.ShapeDtypeStruct((t_all // 128, heads, 128, 128), F32)] * 4
                  + [jax.ShapeDtypeStruct((t_all // 128, heads, 128), F32)],
        compiler_params=_params("parallel"),
        name="peer_route",
    )(q, keys)


GATE_GROUP = 2


def _peer_kernel(t_ref, u_ref, vt_ref, s1_ref, s2_ref, e1_ref, e2_ref, tau_ref, h2_ref, gf_ref,
                 y_ref, acc_ref, gate_ref, *, heads, te):
    j = pl.program_id(1)
    tm = t_ref.shape[0]

    @pl.when(j == 0)
    def _():
        acc_ref[...] = jnp.zeros_like(acc_ref)

    tile = (8, 128)
    n_a = te // 128
    for a8 in range(0, n_a, 8):
        a_rows = pl.ds(pl.multiple_of(j * n_a + a8, 8), 8)
        for ag in range(0, 8, GATE_GROUP):

            def lane_block(lb, carry, a_rows=a_rows, ag=ag, a8=a8):
                acc = [[jnp.zeros(tile, F32) for _ in range(16)] for _ in range(GATE_GROUP)]
                for h in range(heads):
                    tau = jnp.broadcast_to(tau_ref[lb, h:h + 1, :], tile)
                    s1 = s1_ref[lb, h, a_rows, :]
                    e1 = e1_ref[lb, h, a_rows, :]
                    s1b = [jnp.broadcast_to(s1[ag + ai:ag + ai + 1, :], tile) for ai in range(GATE_GROUP)]
                    e1b = [jnp.broadcast_to(e1[ag + ai:ag + ai + 1, :], tile) for ai in range(GATE_GROUP)]
                    for r in range(16):
                        s2 = s2_ref[lb, h, 8 * r:8 * r + 8, :]
                        e2 = e2_ref[lb, h, 8 * r:8 * r + 8, :]
                        for ai in range(GATE_GROUP):
                            acc[ai][r] = acc[ai][r] + jnp.where(s2 + s1b[ai] >= tau, e2, 0.0) * e1b[ai]
                for ai in range(GATE_GROUP):
                    for r in range(16):
                        row = (a8 + ag + ai) * 128 + 8 * r
                        gate_ref[lb, row:row + 8, :] = acc[ai][r]
                return carry

            lax.fori_loop(0, tm // 128, lane_block, 0)

    hp = _dot_nt(u_ref[...], t_ref[...])
    gate = jnp.concatenate([gate_ref[lb] for lb in range(tm // 128)], axis=1)
    act = 0.5 * hp * (1.0 + lax.erf(hp * (1.0 / math.sqrt(2.0)))) * gate
    acc_ref[...] += _dot(vt_ref[...], act.astype(BF16))

    @pl.when(j == pl.num_programs(1) - 1)
    def _():
        y_ref[...] = _rms(h2_ref[...] + acc_ref[...].T, gf_ref[...])


def _peer(t, u, vt, s1, s2, e1, e2, tau, h2, gf, heads, tm, te):
    t_all, d = t.shape
    n_exp = u.shape[0]
    tok = pl.BlockSpec((tm // 128, heads, 128, 128), lambda i, j: (i, 0, 0, 0))
    return pl.pallas_call(
        functools.partial(_peer_kernel, heads=heads, te=te),
        grid=(t_all // tm, n_exp // te),
        in_specs=[pl.BlockSpec((tm, d), lambda i, j: (i, 0)),
                  pl.BlockSpec((te, d), lambda i, j: (j, 0)),
                  pl.BlockSpec((d, te), lambda i, j: (0, j)),
                  tok, tok, tok, tok,
                  pl.BlockSpec((tm // 128, heads, 128), lambda i, j: (i, 0, 0)),
                  pl.BlockSpec((tm, d), lambda i, j: (i, 0)),
                  _full(gf.shape)],
        out_specs=pl.BlockSpec((tm, d), lambda i, j: (i, 0)),
        out_shape=jax.ShapeDtypeStruct((t_all, d), F32),
        scratch_shapes=[pltpu.VMEM((d, tm), F32), pltpu.VMEM((tm // 128, te, 128), F32)],
        compiler_params=_params("parallel", "arbitrary"),
        name="peer_experts",
    )(t, u, vt, s1, s2, e1, e2, tau, h2, gf)


def _tile(n, cap):
    t = cap
    while n % t:
        t //= 2
    return t


def kernel(x, meta_tokens, norm_mix_g, w_in, mla_q_norm_g, mla_w_uq, mla_kv_norm_g, mla_w_ukv, mla_out_g, rwkv_mu, rwkv_w0, rwkv_w_up, rwkv_a0, rwkv_a_up, rwkv_g_up, rwkv_k_k, rwkv_k_a, rwkv_r_k, rwkv_ln_w, rwkv_ln_b, w_out, norm_ffn_g, peer_w_q, peer_sub_keys, peer_u, peer_v, norm_final_g):
    batch, s_len, d = x.shape
    assert w_in.shape[0] == 1, "one layer"
    assert s_len % ATT_Q == 0 and d == 1024
    lp = X0 + s_len
    tp = batch * lp
    row = lambda a: a.reshape(1, -1).astype(F32)

    h = jnp.concatenate([jnp.zeros((batch, META0, d), F32),
                         jnp.broadcast_to(meta_tokens[None].astype(F32), (batch, N_META, d)), x], axis=1)
    h = h.reshape(tp, d)

    wi = w_in[0]
    kr = wi[:, 384:448]
    z64 = jnp.zeros((d, 64), F32)
    w_ext = jnp.concatenate([wi[:, 0:384], kr, z64, kr[:, 32:], kr[:, :32], z64, wi[:, 448:]], axis=1).astype(BF16)
    n_mla = 640
    pm, pr = _inproj(h, row(norm_mix_g[0]), w_ext, n_mla, _tile(tp, 512))

    pos = jnp.maximum(jnp.arange(lp) - META0, 0).astype(F32)
    inv = ROPE_THETA ** (-jnp.arange(32, dtype=F32) / 32)
    ang = pos[:, None] * inv[None, :]
    zl = jnp.zeros((lp, 64), F32)
    cos = jnp.tile(jnp.concatenate([jnp.cos(ang), jnp.cos(ang), zl], axis=1), (batch, 1))
    sin = jnp.tile(jnp.concatenate([-jnp.sin(ang), jnp.sin(ang), zl], axis=1), (batch, 1))

    wuq = mla_w_uq[0].reshape(256, MLA_HEADS, 192)
    zq = jnp.zeros((256, MLA_HEADS, 64), F32)
    wqa = jnp.concatenate([wuq, zq], axis=2).reshape(256, MLA_HEADS * 256).astype(BF16)
    wqb = jnp.concatenate([wuq[:, :, 160:192], wuq[:, :, 128:160], zq], axis=2).reshape(256, MLA_HEADS * 128).astype(BF16)
    wukv = mla_w_ukv[0].reshape(128, MLA_HEADS, 256)
    wk = wukv[:, :, :128].reshape(128, MLA_HEADS * 128).astype(BF16)
    wv = wukv[:, :, 128:].reshape(128, MLA_HEADS * 128).astype(BF16)
    q, k, v = _mla_prep(pm, cos, sin, row(mla_q_norm_g[0]), row(mla_kv_norm_g[0]), wqa, wqb, wk, wv,
                        192 ** -0.5, _tile(tp, 512))
    o_mla = _attention(q, k, v, batch, lp, s_len)

    z_up = jnp.zeros((64, 512), F32)
    wup = jnp.concatenate([rwkv_w_up[0], z_up], axis=0).astype(BF16)
    aup = jnp.concatenate([z_up, rwkv_a_up[0]], axis=0).astype(BF16)
    head_id = jnp.arange(512) // RWKV_HEAD
    bd = (head_id[:, None] == head_id[None, :]).astype(F32)
    r, lw, k2, vv, kn, bv, bonus, g = _rwkv_prep(
        pr, row(rwkv_mu[0]), row(rwkv_w0[0]), row(rwkv_a0[0]), row(rwkv_k_k[0]), row(rwkv_k_a[0]),
        row(rwkv_r_k[0]), wup, aup, rwkv_g_up[0].astype(BF16), bd, batch, lp, 256)
    o_rwkv = _rwkv_scan(r, lw, k2, vv, kn, bv, batch, lp, s_len)

    heads = peer_sub_keys.shape[1]
    h2, t, pq = _outproj(x.reshape(batch * s_len, d), o_mla, o_rwkv, bonus, g, row(mla_out_g[0]),
                         row(rwkv_ln_w[0]), row(rwkv_ln_b[0]), bd, w_out[0].astype(BF16), row(norm_ffn_g[0]),
                         peer_w_q[0].astype(BF16), batch, lp, s_len, _tile(s_len, 512))

    t_all = batch * s_len
    keys = peer_sub_keys[0].reshape(heads * 2, 128, 128).astype(BF16)
    tm = _tile(t_all, 512)
    s1, s2, e1, e2, tau = _route(pq, keys, heads, tm)
    y = _peer(t, peer_u[0].astype(BF16), peer_v[0].T.astype(BF16), s1, s2, e1, e2, tau, h2,
              row(norm_final_g), heads, tm, 1024)
    return y.reshape(batch, s_len, d)
```

```python
import functools
import math

import jax
import jax.numpy as jnp
from jax import lax
from jax.experimental import pallas as pl
from jax.experimental.pallas import tpu as pltpu

F32 = jnp.float32
BF16 = jnp.bfloat16
HIGHEST = lax.Precision.HIGHEST

RMS_EPS = 1e-6
GN_EPS = 64e-5
ROPE_THETA = 10000.0
N_META = 16
CHUNK = 64
X0 = 512
META0 = X0 - N_META
K0 = X0 - 128
SCAN_C0 = (X0 - CHUNK) // CHUNK
MLA_HEADS = 4
RWKV_HEAD = 64
PEER_TOPK = 16
VMEM_LIMIT = 56 * 1024 * 1024


def _dot(a, b, precision=None):
    return jnp.dot(a, b, preferred_element_type=F32, precision=precision)


def _dot_nt(a, b, precision=None):
    return lax.dot_general(a, b, (((1,), (1,)), ((), ())),
                           preferred_element_type=F32, precision=precision)


def _dot_split(x, m):
    hi = x.astype(BF16)
    lo = (x - hi.astype(F32)).astype(BF16)
    return _dot(hi, m) + _dot(lo, m)


def _rms(x, g):
    return x * lax.rsqrt(jnp.mean(x * x, axis=-1, keepdims=True) + RMS_EPS) * g


def _params(*sem):
    return pltpu.CompilerParams(dimension_semantics=sem, vmem_limit_bytes=VMEM_LIMIT)


def _full(shape):
    n = len(shape)
    return pl.BlockSpec(shape, lambda *_: (0,) * n)


def _inproj_kernel(h_ref, g_ref, w_ref, pm_ref, pr_ref, *, n_mla):
    n = _rms(h_ref[...], g_ref[...]).astype(BF16)
    p = _dot(n, w_ref[...])
    pm_ref[...] = p[:, :n_mla]
    pr_ref[...] = p[:, n_mla:]


def _inproj(h, g, w, n_mla, tm):
    tp, d = h.shape
    n_all = w.shape[1]
    return pl.pallas_call(
        functools.partial(_inproj_kernel, n_mla=n_mla),
        grid=(tp // tm,),
        in_specs=[pl.BlockSpec((tm, d), lambda i: (i, 0)), _full((1, d)), _full(w.shape)],
        out_specs=[pl.BlockSpec((tm, n_mla), lambda i: (i, 0)),
                   pl.BlockSpec((tm, n_all - n_mla), lambda i: (i, 0))],
        out_shape=[jax.ShapeDtypeStruct((tp, n_mla), F32),
                   jax.ShapeDtypeStruct((tp, n_all - n_mla), F32)],
        compiler_params=_params("parallel"),
        name="inproj",
    )(h, g, w)


def _mla_prep_kernel(pm_ref, cos_ref, sin_ref, gq_ref, gkv_ref, wqa_ref, wqb_ref, wk_ref, wv_ref,
                     q_ref, k_ref, v_ref, *, scale):
    pm = pm_ref[...]
    nq = _rms(pm[:, 0:256], gq_ref[...]).astype(BF16)
    nkv = _rms(pm[:, 256:384], gkv_ref[...]).astype(BF16)
    cos, sin = cos_ref[...], sin_ref[...]
    qa = _dot(nq, wqa_ref[...])
    qb = _dot(nq, wqb_ref[...])
    kn = _dot(nkv, wk_ref[...])
    vv = _dot(nkv, wv_ref[...])
    kr = pm[:, 384:512] * cos + pm[:, 512:640] * sin
    for h in range(MLA_HEADS):
        qn = qa[:, 256 * h:256 * h + 128]
        qr = qa[:, 256 * h + 128:256 * h + 256] * cos + qb[:, 128 * h:128 * h + 128] * sin
        q_ref[h] = (jnp.concatenate([qn, qr], axis=-1) * scale).astype(BF16)
        k_ref[h] = jnp.concatenate([kn[:, 128 * h:128 * h + 128], kr], axis=-1).astype(BF16)
        v_ref[h] = vv[:, 128 * h:128 * h + 128].astype(BF16)


def _mla_prep(pm, cos, sin, gq, gkv, wqa, wqb, wk, wv, scale, tm):
    tp = pm.shape[0]
    row = lambda w: pl.BlockSpec((tm, w), lambda i: (i, 0))
    head = lambda w: pl.BlockSpec((MLA_HEADS, tm, w), lambda i: (0, i, 0))
    return pl.pallas_call(
        functools.partial(_mla_prep_kernel, scale=scale),
        grid=(tp // tm,),
        in_specs=[row(pm.shape[1]), row(128), row(128), _full(gq.shape), _full(gkv.shape),
                  _full(wqa.shape), _full(wqb.shape), _full(wk.shape), _full(wv.shape)],
        out_specs=[head(256), head(256), head(128)],
        out_shape=[jax.ShapeDtypeStruct((MLA_HEADS, tp, 256), BF16),
                   jax.ShapeDtypeStruct((MLA_HEADS, tp, 256), BF16),
                   jax.ShapeDtypeStruct((MLA_HEADS, tp, 128), BF16)],
        compiler_params=_params("parallel"),
        name="mla_prep",
    )(pm, cos, sin, gq, gkv, wqa, wqb, wk, wv)


ATT_Q = 256


def _attn_kernel(q_ref, k_ref, v_ref, o_ref, *, n_q):
    for qi in range(n_q):
        q0 = X0 + ATT_Q * qi
        kend = q0 + ATT_Q
        n = kend - K0
        q = q_ref[q0:q0 + ATT_Q, :]
        s = _dot_nt(q, k_ref[K0:kend, :])
        kk = lax.broadcasted_iota(jnp.int32, (1, n), 1)
        ck = jnp.where(kk < 128 - N_META, 1 << 20, kk >> 6)
        cq = ((ATT_Q * qi + lax.broadcasted_iota(jnp.int32, (ATT_Q, 1), 0)) >> 6) + 2
        s = jnp.where(ck <= cq, s, -1e30)
        e = jnp.exp(s - jnp.max(s, axis=-1, keepdims=True))
        l = jnp.sum(e, axis=-1, keepdims=True)
        o = _dot(e.astype(BF16), v_ref[K0:kend, :])
        o_ref[ATT_Q * qi:ATT_Q * (qi + 1), :] = o / l


def _attention(q, k, v, batch, lp, s_len):
    return pl.pallas_call(
        functools.partial(_attn_kernel, n_q=s_len // ATT_Q),
        grid=(batch, MLA_HEADS),
        in_specs=[pl.BlockSpec((None, lp, 256), lambda b, h: (h, b, 0)),
                  pl.BlockSpec((None, lp, 256), lambda b, h: (h, b, 0)),
                  pl.BlockSpec((None, lp, 128), lambda b, h: (h, b, 0))],
        out_specs=pl.BlockSpec((s_len, 128), lambda b, h: (b, h)),
        out_shape=jax.ShapeDtypeStruct((batch * s_len, MLA_HEADS * 128), F32),
        compiler_params=_params("parallel", "parallel"),
        name="mla_attn",
    )(q, k, v)


def _softplus(x):
    return jnp.maximum(x, 0.0) + jnp.log(1.0 + jnp.exp(-jnp.abs(x)))


def _sigmoid(x):
    return 1.0 / (1.0 + jnp.exp(-x))


def _rwkv_prep_kernel(p_ref, prev_ref, mu_ref, w0_ref, a0_ref, kk_ref, ka_ref, rk_ref,
                      wup_ref, aup_ref, gup_ref, bd_ref,
                      r_ref, lw_ref, k_ref, v_ref, kn_ref, b_ref, bonus_ref, g_ref, *, tm):
    p = p_ref[...]
    rows = lax.broadcasted_iota(jnp.int32, (tm, 1), 0)
    shifted = jnp.where(rows == 0, prev_ref[7:8, :], pltpu.roll(p, 1, 0))
    pm = p + mu_ref[...] * (shifted - p)
    r, k, v = pm[:, 0:512], pm[:, 512:1024], pm[:, 1024:1536]
    dd, dg = pm[:, 1536:1664], pm[:, 1664:1792]
    w = -_softplus(-(w0_ref[...] + _dot(jnp.tanh(dd).astype(BF16), wup_ref[...]))) - 0.5
    log_decay = -jnp.exp(w)
    a = _sigmoid(a0_ref[...] + _dot(dd.astype(BF16), aup_ref[...]))
    g_ref[...] = _dot(_sigmoid(dg).astype(BF16), gup_ref[...])
    bd = bd_ref[...]
    kk = k * kk_ref[...]
    norm = jnp.sqrt(_dot_split(kk * kk, bd))
    kk = kk / jnp.maximum(norm, 1e-12)
    k2 = k * (1.0 + (a - 1.0) * ka_ref[...])
    real = (pl.program_id(1) * tm + rows) >= META0
    r_ref[...] = r
    lw_ref[...] = jnp.where(real, log_decay, 0.0)
    k_ref[...] = k2
    v_ref[...] = v
    kn_ref[...] = kk
    b_ref[...] = kk * a
    bonus_ref[...] = _dot_split(r * k2 * rk_ref[...], bd) * v


def _rwkv_prep(pr, mu, w0, a0, k_k, k_a, r_k, wup, aup, gup, bd, batch, lp, tm):
    tp, width = pr.shape
    nl = lp // tm
    vec = lambda a: _full(a.shape)
    out = pl.BlockSpec((tm, 512), lambda b, j: (b * nl + j, 0))
    return pl.pallas_call(
        functools.partial(_rwkv_prep_kernel, tm=tm),
        grid=(batch, nl),
        in_specs=[pl.BlockSpec((tm, width), lambda b, j: (b * nl + j, 0)),
                  pl.BlockSpec((8, width), lambda b, j: (jnp.maximum((b * nl + j) * (tm // 8) - 1, 0), 0)),
                  vec(mu), vec(w0), vec(a0), vec(k_k), vec(k_a), vec(r_k),
                  vec(wup), vec(aup), vec(gup), vec(bd)],
        out_specs=[out] * 8,
        out_shape=[jax.ShapeDtypeStruct((tp, 512), F32)] * 8,
        compiler_params=_params("parallel", "parallel"),
        name="rwkv_prep",
    )(pr, pr, mu, w0, a0, k_k, k_a, r_k, wup, aup, gup, bd)


def _bdot(a, b):
    return _dot(a.astype(BF16), b.astype(BF16))


def _rwkv_scan_kernel(r_ref, lw_ref, k_ref, v_ref, kn_ref, b_ref, o_ref, z_ref):
    c = CHUNK
    w = 4 * c

    @pl.when(pl.program_id(1) == 0)
    def _():
        z_ref[...] = jnp.zeros_like(z_ref)

    ri = lax.broadcasted_iota(jnp.int32, (w, w), 0)
    ci = lax.broadcasted_iota(jnp.int32, (w, w), 1)
    same_head = (ri >> 6) == (ci >> 6)
    strict = (ri & 63) > (ci & 63)
    incl = (ri & 63) >= (ci & 63)
    eye = ri == ci
    tri = (lax.broadcasted_iota(jnp.int32, (c, c), 0) >= lax.broadcasted_iota(jnp.int32, (c, c), 1)).astype(F32)

    def stack(x):
        return jnp.where(same_head, jnp.concatenate([x, x, x, x], axis=0), 0.0)

    lw_all = lw_ref[...]
    cum_all = _dot(tri, lw_all, HIGHEST)
    for q in range(z_ref.shape[0]):
        lanes = slice(w * q, w * q + w)
        lw = lw_all[:, lanes]
        cum = cum_all[:, lanes]
        cum_prev = cum - lw
        mid = cum[c // 2 - 1:c // 2, :]
        tot = cum[c - 1:c, :]
        a = -kn_ref[:, lanes]
        b = b_ref[:, lanes]
        k = k_ref[:, lanes]
        r = r_ref[:, lanes]
        dn = jnp.exp(mid - cum)
        de = jnp.exp(tot - cum)
        a_t = stack(a * jnp.exp(cum_prev - mid))
        a_0 = stack(a * jnp.exp(cum_prev))
        b_t = stack(b * dn)
        k_t = stack(k * dn)
        r_t = stack(r * jnp.exp(cum - mid))
        r_0 = stack(r * jnp.exp(cum))
        b_e = stack(b * de)
        k_e = stack(k * de)
        vs = stack(v_ref[:, lanes])

        inter = _dot_nt(jnp.concatenate([a_t, r_t], axis=0).astype(BF16),
                        jnp.concatenate([b_t, k_t], axis=0).astype(BF16))
        l_ab = jnp.where(strict, inter[:w, :w], 0.0)
        l_ak = jnp.where(strict, inter[:w, w:], 0.0)
        m_rb = jnp.where(incl, inter[w:, :w], 0.0)
        m_rk = jnp.where(incl, inter[w:, w:], 0.0)

        t_inv = jnp.where(eye, 1.0, 0.0) + l_ab
        power = _bdot(l_ab, l_ab)
        for _ in range(int(math.log2(c)) - 2):
            both = _bdot(power, jnp.concatenate([power, t_inv], axis=1))
            power, t_inv = both[:, :w], t_inv + both[:, w:]
        t_inv = t_inv + _bdot(power, t_inv)

        pq = _bdot(t_inv, jnp.concatenate([a_0, l_ak], axis=1))
        mix = _bdot(jnp.concatenate([m_rb, b_e.T], axis=0), pq)
        r_hat = r_0 + mix[:w, :w]
        m_o = mix[:w, w:] + m_rk
        g = jnp.where(eye, jnp.exp(tot), 0.0) + mix[w:, :w]
        m_h = mix[w:, w:] + k_e.T
        lhs = jnp.concatenate([jnp.concatenate([r_hat, m_o], axis=1), jnp.concatenate([g, m_h], axis=1)], axis=0)
        new = _bdot(lhs, jnp.concatenate([z_ref[q], vs], axis=0))
        o_rows = new[:w]
        o_ref[:, lanes] = o_rows[0:c] + o_rows[c:2 * c] + o_rows[2 * c:3 * c] + o_rows[3 * c:4 * c]
        z_ref[q] = new[w:]


def _rwkv_scan(r, lw, k, v, kn, b, batch, lp, s_len):
    nc = s_len // CHUNK + 1
    rows_per_b = lp // CHUNK
    nx = s_len // CHUNK
    blk = pl.BlockSpec((CHUNK, 512), lambda bi, ci: (bi * rows_per_b + SCAN_C0 + ci, 0))
    return pl.pallas_call(
        _rwkv_scan_kernel,
        grid=(batch, nc),
        in_specs=[blk] * 6,
        out_specs=pl.BlockSpec((CHUNK, 512), lambda bi, ci: (bi * nx + jnp.maximum(ci - 1, 0), 0)),
        out_shape=jax.ShapeDtypeStruct((batch * s_len, 512), F32),
        scratch_shapes=[pltpu.VMEM((2, 256, 256), F32)],
        compiler_params=_params("parallel", "arbitrary"),
        name="rwkv_scan",
    )(r, lw, k, v, kn, b)


def _outproj_kernel(x_ref, om_ref, orw_ref, bonus_ref, g_ref, og_ref, lnw_ref, lnb_ref, bd_ref,
                    wout_ref, gffn_ref, wq_ref, h2_ref, t_ref, q_ref):
    y_mla = _rms(om_ref[...], og_ref[...])
    o = orw_ref[...]
    bd = bd_ref[...]
    mean = _dot_split(o, bd) * (1.0 / RWKV_HEAD)
    cen = o - mean
    var = _dot_split(cen * cen, bd) * (1.0 / RWKV_HEAD)
    y_rwkv = (cen * lax.rsqrt(var + GN_EPS) * lnw_ref[...] + lnb_ref[...] + bonus_ref[...]) * g_ref[...]
    y = jnp.concatenate([y_mla, y_rwkv], axis=-1).astype(BF16)
    h2 = x_ref[...] + _dot(y, wout_ref[...])
    h2_ref[...] = h2
    t = _rms(h2, gffn_ref[...]).astype(BF16)
    t_ref[...] = t
    q_ref[...] = _dot(t, wq_ref[...])


def _outproj(x, o_mla, o_rwkv, bonus, g, og, lnw, lnb, bd, wout, gffn, wq, batch, lp, s_len, tm):
    t_all, d = x.shape
    ns = s_len // tm
    npad = lp // tm
    xrow = lambda w: pl.BlockSpec((tm, w), lambda b, j: (b * ns + j, 0))
    prow = lambda w: pl.BlockSpec((tm, w), lambda b, j: (b * npad + X0 // tm + j, 0))
    vec = lambda a: _full(a.shape)
    return pl.pallas_call(
        _outproj_kernel,
        grid=(batch, ns),
        in_specs=[xrow(d), xrow(512), xrow(512), prow(512), prow(512), vec(og), vec(lnw), vec(lnb), vec(bd),
                  vec(wout), vec(gffn), vec(wq)],
        out_specs=[xrow(d), xrow(d), xrow(wq.shape[1])],
        out_shape=[jax.ShapeDtypeStruct((t_all, d), F32), jax.ShapeDtypeStruct((t_all, d), BF16),
                   jax.ShapeDtypeStruct((t_all, wq.shape[1]), F32)],
        compiler_params=_params("parallel", "parallel"),
        name="outproj",
    )(x, o_mla, o_rwkv, bonus, g, og, lnw, lnb, bd, wout, gffn, wq)


_CANDS = [(i, j) for i in range(PEER_TOPK) for j in range(PEER_TOPK) if (i + 1) * (j + 1) <= PEER_TOPK]


def _exchange(v, i, l):
    v[i], v[l] = jnp.maximum(v[i], v[l]), jnp.minimum(v[i], v[l])


def _bitonic_merge_desc(v):
    n = len(v)
    j = n // 2
    while j >= 1:
        for i in range(n):
            if i ^ j > i:
                _exchange(v, i, i ^ j)
        j //= 2
    return v


def _bitonic_sort_desc(v):
    n = len(v)
    k = 2
    while k <= n:
        j = k // 2
        while j >= 1:
            for i in range(n):
                l = i ^ j
                if l > i:
                    if (i & k) == 0 or k == n:
                        _exchange(v, i, l)
                    else:
                        _exchange(v, l, i)
            j //= 2
        k *= 2
    return v


def _top16_rows(s):
    v = _bitonic_sort_desc([s[8 * k:8 * k + 8, :] for k in range(PEER_TOPK)])
    for shift in (4, 2, 1):
        other = [pltpu.roll(x, shift, 0) for x in v]
        v = _bitonic_merge_desc([jnp.maximum(v[i], other[PEER_TOPK - 1 - i]) for i in range(PEER_TOPK)])
    return v


def _top_distinct(x, weight, n):
    out = []
    for _ in range(n):
        m = jnp.max(x, axis=0, keepdims=True)
        eq = x == m
        cnt = jnp.sum(jnp.where(eq, weight, 0.0), axis=0, keepdims=True)
        x = jnp.where(eq, -jnp.inf, x)
        out.append((m, cnt))
    return out


def _route_kernel(q_ref, keys_ref, s1_ref, s2_ref, e1_ref, e2_ref, tau_ref, *, heads, tm):
    qb = q_ref[...].astype(BF16)
    pad = 56 - len(_CANDS)
    neg = jnp.full((pad, tm), -jnp.inf, F32)
    cw = jnp.concatenate([jnp.ones((len(_CANDS), tm), F32), jnp.zeros((pad, tm), F32)], axis=0)
    for h in range(heads):
        s = [_dot_nt(keys_ref[2 * h + c], qb[:, 128 * (2 * h + c):128 * (2 * h + c + 1)]) for c in range(2)]
        top = [[x[0:1, :] for x in _top16_rows(s[c])] for c in range(2)]
        cv = jnp.concatenate([top[0][i] + top[1][j] for i, j in _CANDS] + [neg], axis=0)
        tau = jnp.full((1, tm), -jnp.inf, F32)
        seen = jnp.zeros((1, tm), F32)
        for m, cnt in _top_distinct(cv, cw, PEER_TOPK):
            seen = seen + cnt
            tau = jnp.maximum(tau, jnp.where(seen >= PEER_TOPK, m, -jnp.inf))
        m1, m2 = top[0][0], top[1][0]
        z = jnp.sum(jnp.where(cv >= tau, cw * jnp.exp(cv - (m1 + m2)), 0.0), axis=0, keepdims=True)
        e1 = jnp.exp(s[0] - m1) / z
        e2 = jnp.exp(s[1] - m2)
        for lb in range(tm // 128):
            lanes = slice(128 * lb, 128 * lb + 128)
            s1_ref[lb, h] = s[0][:, lanes]
            s2_ref[lb, h] = s[1][:, lanes]
            e1_ref[lb, h] = e1[:, lanes]
            e2_ref[lb, h] = e2[:, lanes]
            tau_ref[lb, h:h + 1, :] = tau[:, lanes]


def _route(q, keys, heads, tm):
    t_all = q.shape[0]
    big = pl.BlockSpec((tm // 128, heads, 128, 128), lambda i: (i, 0, 0, 0))
    return pl.pallas_call(
        functools.partial(_route_kernel, heads=heads, tm=tm),
        grid=(t_all // tm,),
        in_specs=[pl.BlockSpec((tm, q.shape[1]), lambda i: (i, 0)), _full(keys.shape)],
        out_specs=[big] * 4 + [pl.BlockSpec((tm // 128, heads, 128), lambda i: (i, 0, 0))],
        out_shape=[jax.ShapeDtypeStruct((t_all // 128, heads, 128, 128), F32)] * 4
                  + [jax.ShapeDtypeStruct((t_all // 128, heads, 128), F32)],
        compiler_params=_params("parallel"),
        name="peer_route",
    )(q, keys)


GATE_GROUP = 2


def _peer_kernel(t_ref, u_ref, vt_ref, s1_ref, s2_ref, e1_ref, e2_ref, tau_ref, h2_ref, gf_ref,
                 y_ref, acc_ref, gate_ref, *, heads, te):
    j = pl.program_id(1)
    tm = t_ref.shape[0]

    @pl.when(j == 0)
    def _():
        acc_ref[...] = jnp.zeros_like(acc_ref)

    tile = (8, 128)
    n_a = te // 128
    for a8 in range(0, n_a, 8):
        a_rows = pl.ds(pl.multiple_of(j * n_a + a8, 8), 8)
        for ag in range(0, 8, GATE_GROUP):

            def lane_block(lb, carry, a_rows=a_rows, ag=ag, a8=a8):
                acc = [[jnp.zeros(tile, F32) for _ in range(16)] for _ in range(GATE_GROUP)]
                for h in range(heads):
                    tau = jnp.broadcast_to(tau_ref[lb, h:h + 1, :], tile)
                    s1 = s1_ref[lb, h, a_rows, :]
                    e1 = e1_ref[lb, h, a_rows, :]
                    s1b = [jnp.broadcast_to(s1[ag + ai:ag + ai + 1, :], tile) for ai in range(GATE_GROUP)]
                    e1b = [jnp.broadcast_to(e1[ag + ai:ag + ai + 1, :], tile) for ai in range(GATE_GROUP)]
                    for r in range(16):
                        s2 = s2_ref[lb, h, 8 * r:8 * r + 8, :]
                        e2 = e2_ref[lb, h, 8 * r:8 * r + 8, :]
                        for ai in range(GATE_GROUP):
                            acc[ai][r] = acc[ai][r] + jnp.where(s2 + s1b[ai] >= tau, e2, 0.0) * e1b[ai]
                for ai in range(GATE_GROUP):
                    for r in range(16):
                        row = (a8 + ag + ai) * 128 + 8 * r
                        gate_ref[lb, row:row + 8, :] = acc[ai][r]
                return carry

            lax.fori_loop(0, tm // 128, lane_block, 0)

    hp = _dot_nt(u_ref[...], t_ref[...])
    gate = jnp.concatenate([gate_ref[lb] for lb in range(tm // 128)], axis=1)
    act = 0.5 * hp * (1.0 + lax.erf(hp * (1.0 / math.sqrt(2.0)))) * gate
    acc_ref[...] += _dot(vt_ref[...], act.astype(BF16))

    @pl.when(j == pl.num_programs(1) - 1)
    def _():
        y_ref[...] = _rms(h2_ref[...] + acc_ref[...].T, gf_ref[...])


def _peer(t, u, vt, s1, s2, e1, e2, tau, h2, gf, heads, tm, te):
    t_all, d = t.shape
    n_exp = u.shape[0]
    tok = pl.BlockSpec((tm // 128, heads, 128, 128), lambda i, j: (i, 0, 0, 0))
    return pl.pallas_call(
        functools.partial(_peer_kernel, heads=heads, te=te),
        grid=(t_all // tm, n_exp // te),
        in_specs=[pl.BlockSpec((tm, d), lambda i, j: (i, 0)),
                  pl.BlockSpec((te, d), lambda i, j: (j, 0)),
                  pl.BlockSpec((d, te), lambda i, j: (0, j)),
                  tok, tok, tok, tok,
                  pl.BlockSpec((tm // 128, heads, 128), lambda i, j: (i, 0, 0)),
                  pl.BlockSpec((tm, d), lambda i, j: (i, 0)),
                  _full(gf.shape)],
        out_specs=pl.BlockSpec((tm, d), lambda i, j: (i, 0)),
        out_shape=jax.ShapeDtypeStruct((t_all, d), F32),
        scratch_shapes=[pltpu.VMEM((d, tm), F32), pltpu.VMEM((tm // 128, te, 128), F32)],
        compiler_params=_params("parallel", "arbitrary"),
        name="peer_experts",
    )(t, u, vt, s1, s2, e1, e2, tau, h2, gf)


def _tile(n, cap):
    t = cap
    while n % t:
        t //= 2
    return t


def kernel(x, meta_tokens, norm_mix_g, w_in, mla_q_norm_g, mla_w_uq, mla_kv_norm_g, mla_w_ukv, mla_out_g, rwkv_mu, rwkv_w0, rwkv_w_up, rwkv_a0, rwkv_a_up, rwkv_g_up, rwkv_k_k, rwkv_k_a, rwkv_r_k, rwkv_ln_w, rwkv_ln_b, w_out, norm_ffn_g, peer_w_q, peer_sub_keys, peer_u, peer_v, norm_final_g):
    batch, s_len, d = x.shape
    assert w_in.shape[0] == 1, "one layer"
    assert s_len % ATT_Q == 0 and d == 1024
    lp = X0 + s_len
    tp = batch * lp
    row = lambda a: a.reshape(1, -1).astype(F32)

    h = jnp.concatenate([jnp.zeros((batch, META0, d), F32),
                         jnp.broadcast_to(meta_tokens[None].astype(F32), (batch, N_META, d)), x], axis=1)
    h = h.reshape(tp, d)

    wi = w_in[0]
    kr = wi[:, 384:448]
    z64 = jnp.zeros((d, 64), F32)
    w_ext = jnp.concatenate([wi[:, 0:384], kr, z64, kr[:, 32:], kr[:, :32], z64, wi[:, 448:]], axis=1).astype(BF16)
    n_mla = 640
    pm, pr = _inproj(h, row(norm_mix_g[0]), w_ext, n_mla, _tile(tp, 512))

    pos = jnp.maximum(jnp.arange(lp) - META0, 0).astype(F32)
    inv = ROPE_THETA ** (-jnp.arange(32, dtype=F32) / 32)
    ang = pos[:, None] * inv[None, :]
    zl = jnp.zeros((lp, 64), F32)
    cos = jnp.tile(jnp.concatenate([jnp.cos(ang), jnp.cos(ang), zl], axis=1), (batch, 1))
    sin = jnp.tile(jnp.concatenate([-jnp.sin(ang), jnp.sin(ang), zl], axis=1), (batch, 1))

    wuq = mla_w_uq[0].reshape(256, MLA_HEADS, 192)
    zq = jnp.zeros((256, MLA_HEADS, 64), F32)
    wqa = jnp.concatenate([wuq, zq], axis=2).reshape(256, MLA_HEADS * 256).astype(BF16)
    wqb = jnp.concatenate([wuq[:, :, 160:192], wuq[:, :, 128:160], zq], axis=2).reshape(256, MLA_HEADS * 128).astype(BF16)
    wukv = mla_w_ukv[0].reshape(128, MLA_HEADS, 256)
    wk = wukv[:, :, :128].reshape(128, MLA_HEADS * 128).astype(BF16)
    wv = wukv[:, :, 128:].reshape(128, MLA_HEADS * 128).astype(BF16)
    q, k, v = _mla_prep(pm, cos, sin, row(mla_q_norm_g[0]), row(mla_kv_norm_g[0]), wqa, wqb, wk, wv,
                        192 ** -0.5, _tile(tp, 512))
    o_mla = _attention(q, k, v, batch, lp, s_len)

    z_up = jnp.zeros((64, 512), F32)
    wup = jnp.concatenate([rwkv_w_up[0], z_up], axis=0).astype(BF16)
    aup = jnp.concatenate([z_up, rwkv_a_up[0]], axis=0).astype(BF16)
    head_id = jnp.arange(512) // RWKV_HEAD
    bd = (head_id[:, None] == head_id[None, :]).astype(BF16)
    r, lw, k2, vv, kn, bv, bonus, g = _rwkv_prep(
        pr, row(rwkv_mu[0]), row(rwkv_w0[0]), row(rwkv_a0[0]), row(rwkv_k_k[0]), row(rwkv_k_a[0]),
        row(rwkv_r_k[0]), wup, aup, rwkv_g_up[0].astype(BF16), bd, batch, lp, 256)
    o_rwkv = _rwkv_scan(r, lw, k2, vv, kn, bv, batch, lp, s_len)

    heads = peer_sub_keys.shape[1]
    h2, t, pq = _outproj(x.reshape(batch * s_len, d), o_mla, o_rwkv, bonus, g, row(mla_out_g[0]),
                         row(rwkv_ln_w[0]), row(rwkv_ln_b[0]), bd, w_out[0].astype(BF16), row(norm_ffn_g[0]),
                         peer_w_q[0].astype(BF16), batch, lp, s_len, _tile(s_len, 512))

    t_all = batch * s_len
    keys = peer_sub_keys[0].reshape(heads * 2, 128, 128).astype(BF16)
    tm = _tile(t_all, 512)
    s1, s2, e1, e2, tau = _route(pq, keys, heads, tm)
    y = _peer(t, peer_u[0].astype(BF16), peer_v[0].T.astype(BF16), s1, s2, e1, e2, tau, h2,
              row(norm_final_g), heads, tm, 1024)
    return y.reshape(batch, s_len, d)
```

```python
import functools
import math

import jax
import jax.numpy as jnp
from jax import lax
from jax.experimental import pallas as pl
from jax.experimental.pallas import tpu as pltpu

F32 = jnp.float32
BF16 = jnp.bfloat16
HIGHEST = lax.Precision.HIGHEST

RMS_EPS = 1e-6
GN_EPS = 64e-5
ROPE_THETA = 10000.0
N_META = 16
CHUNK = 64
X0 = 512
META0 = X0 - N_META
K0 = X0 - 128
SCAN_C0 = (X0 - CHUNK) // CHUNK
MLA_HEADS = 4
RWKV_HEAD = 64
PEER_TOPK = 16
VMEM_LIMIT = 56 * 1024 * 1024


def _dot(a, b, precision=None):
    return jnp.dot(a, b, preferred_element_type=F32, precision=precision)


def _dot_nt(a, b, precision=None):
    return lax.dot_general(a, b, (((1,), (1,)), ((), ())),
                           preferred_element_type=F32, precision=precision)


def _dot_split(x, m):
    hi = x.astype(BF16)
    lo = (x - hi.astype(F32)).astype(BF16)
    return _dot(hi, m) + _dot(lo, m)


def _rms(x, g):
    return x * lax.rsqrt(jnp.mean(x * x, axis=-1, keepdims=True) + RMS_EPS) * g


def _params(*sem):
    return pltpu.CompilerParams(dimension_semantics=sem, vmem_limit_bytes=VMEM_LIMIT)


def _full(shape):
    n = len(shape)
    return pl.BlockSpec(shape, lambda *_: (0,) * n)


def _inproj_kernel(h_ref, g_ref, w_ref, pm_ref, pr_ref, *, n_mla):
    n = _rms(h_ref[...], g_ref[...]).astype(BF16)
    p = _dot(n, w_ref[...])
    pm_ref[...] = p[:, :n_mla]
    pr_ref[...] = p[:, n_mla:]


def _inproj(h, g, w, n_mla, tm):
    tp, d = h.shape
    n_all = w.shape[1]
    return pl.pallas_call(
        functools.partial(_inproj_kernel, n_mla=n_mla),
        grid=(tp // tm,),
        in_specs=[pl.BlockSpec((tm, d), lambda i: (i, 0)), _full((1, d)), _full(w.shape)],
        out_specs=[pl.BlockSpec((tm, n_mla), lambda i: (i, 0)),
                   pl.BlockSpec((tm, n_all - n_mla), lambda i: (i, 0))],
        out_shape=[jax.ShapeDtypeStruct((tp, n_mla), F32),
                   jax.ShapeDtypeStruct((tp, n_all - n_mla), F32)],
        compiler_params=_params("parallel"),
        name="inproj",
    )(h, g, w)


def _mla_prep_kernel(pm_ref, cos_ref, sin_ref, gq_ref, gkv_ref, wqa_ref, wqb_ref, wk_ref, wv_ref,
                     q_ref, k_ref, v_ref, *, scale):
    pm = pm_ref[...]
    nq = _rms(pm[:, 0:256], gq_ref[...]).astype(BF16)
    nkv = _rms(pm[:, 256:384], gkv_ref[...]).astype(BF16)
    cos, sin = cos_ref[...], sin_ref[...]
    qa = _dot(nq, wqa_ref[...])
    qb = _dot(nq, wqb_ref[...])
    kn = _dot(nkv, wk_ref[...])
    vv = _dot(nkv, wv_ref[...])
    kr = pm[:, 384:512] * cos + pm[:, 512:640] * sin
    for h in range(MLA_HEADS):
        qn = qa[:, 256 * h:256 * h + 128]
        qr = qa[:, 256 * h + 128:256 * h + 256] * cos + qb[:, 128 * h:128 * h + 128] * sin
        q_ref[h] = (jnp.concatenate([qn, qr], axis=-1) * scale).astype(BF16)
        k_ref[h] = jnp.concatenate([kn[:, 128 * h:128 * h + 128], kr], axis=-1).astype(BF16)
        v_ref[h] = vv[:, 128 * h:128 * h + 128].astype(BF16)


def _mla_prep(pm, cos, sin, gq, gkv, wqa, wqb, wk, wv, scale, tm):
    tp = pm.shape[0]
    row = lambda w: pl.BlockSpec((tm, w), lambda i: (i, 0))
    head = lambda w: pl.BlockSpec((MLA_HEADS, tm, w), lambda i: (0, i, 0))
    return pl.pallas_call(
        functools.partial(_mla_prep_kernel, scale=scale),
        grid=(tp // tm,),
        in_specs=[row(pm.shape[1]), row(128), row(128), _full(gq.shape), _full(gkv.shape),
                  _full(wqa.shape), _full(wqb.shape), _full(wk.shape), _full(wv.shape)],
        out_specs=[head(256), head(256), head(128)],
        out_shape=[jax.ShapeDtypeStruct((MLA_HEADS, tp, 256), BF16),
                   jax.ShapeDtypeStruct((MLA_HEADS, tp, 256), BF16),
                   jax.ShapeDtypeStruct((MLA_HEADS, tp, 128), BF16)],
        compiler_params=_params("parallel"),
        name="mla_prep",
    )(pm, cos, sin, gq, gkv, wqa, wqb, wk, wv)


ATT_Q = 256


def _attn_kernel(q_ref, k_ref, v_ref, o_ref, *, n_q):
    for qi in range(n_q):
        q0 = X0 + ATT_Q * qi
        kend = q0 + ATT_Q
        n = kend - K0
        q = q_ref[q0:q0 + ATT_Q, :]
        s = _dot_nt(q, k_ref[K0:kend, :])
        kk = lax.broadcasted_iota(jnp.int32, (1, n), 1)
        ck = jnp.where(kk < 128 - N_META, 1 << 20, kk >> 6)
        cq = ((ATT_Q * qi + lax.broadcasted_iota(jnp.int32, (ATT_Q, 1), 0)) >> 6) + 2
        s = jnp.where(ck <= cq, s, -1e30)
        e = jnp.exp(s - jnp.max(s, axis=-1, keepdims=True))
        l = jnp.sum(e, axis=-1, keepdims=True)
        o = _dot(e.astype(BF16), v_ref[K0:kend, :])
        o_ref[ATT_Q * qi:ATT_Q * (qi + 1), :] = o / l


def _attention(q, k, v, batch, lp, s_len):
    return pl.pallas_call(
        functools.partial(_attn_kernel, n_q=s_len // ATT_Q),
        grid=(batch, MLA_HEADS),
        in_specs=[pl.BlockSpec((None, lp, 256), lambda b, h: (h, b, 0)),
                  pl.BlockSpec((None, lp, 256), lambda b, h: (h, b, 0)),
                  pl.BlockSpec((None, lp, 128), lambda b, h: (h, b, 0))],
        out_specs=pl.BlockSpec((s_len, 128), lambda b, h: (b, h)),
        out_shape=jax.ShapeDtypeStruct((batch * s_len, MLA_HEADS * 128), F32),
        compiler_params=_params("parallel", "parallel"),
        name="mla_attn",
    )(q, k, v)


def _softplus(x):
    return jnp.maximum(x, 0.0) + jnp.log(1.0 + jnp.exp(-jnp.abs(x)))


def _sigmoid(x):
    return 1.0 / (1.0 + jnp.exp(-x))


def _rwkv_prep_kernel(p_ref, prev_ref, mu_ref, w0_ref, a0_ref, kk_ref, ka_ref, rk_ref,
                      wup_ref, aup_ref, gup_ref, bd_ref,
                      r_ref, lw_ref, k_ref, v_ref, kn_ref, b_ref, bonus_ref, g_ref, *, tm):
    p = p_ref[...]
    rows = lax.broadcasted_iota(jnp.int32, (tm, 1), 0)
    shifted = jnp.where(rows == 0, prev_ref[7:8, :], pltpu.roll(p, 1, 0))
    pm = p + mu_ref[...] * (shifted - p)
    r, k, v = pm[:, 0:512], pm[:, 512:1024], pm[:, 1024:1536]
    dd, dg = pm[:, 1536:1664], pm[:, 1664:1792]
    w = -_softplus(-(w0_ref[...] + _dot(jnp.tanh(dd).astype(BF16), wup_ref[...]))) - 0.5
    log_decay = -jnp.exp(w)
    a = _sigmoid(a0_ref[...] + _dot(dd.astype(BF16), aup_ref[...]))
    g_ref[...] = _dot(_sigmoid(dg).astype(BF16), gup_ref[...])
    bd = bd_ref[...]
    kk = k * kk_ref[...]
    norm = jnp.sqrt(_dot_split(kk * kk, bd))
    kk = kk / jnp.maximum(norm, 1e-12)
    k2 = k * (1.0 + (a - 1.0) * ka_ref[...])
    real = (pl.program_id(1) * tm + rows) >= META0
    r_ref[...] = r
    lw_ref[...] = jnp.where(real, log_decay, 0.0)
    k_ref[...] = k2
    v_ref[...] = v
    kn_ref[...] = kk
    b_ref[...] = kk * a
    bonus_ref[...] = _dot_split(r * k2 * rk_ref[...], bd) * v


def _rwkv_prep(pr, mu, w0, a0, k_k, k_a, r_k, wup, aup, gup, bd, batch, lp, tm):
    tp, width = pr.shape
    nl = lp // tm
    vec = lambda a: _full(a.shape)
    out = pl.BlockSpec((tm, 512), lambda b, j: (b * nl + j, 0))
    return pl.pallas_call(
        functools.partial(_rwkv_prep_kernel, tm=tm),
        grid=(batch, nl),
        in_specs=[pl.BlockSpec((tm, width), lambda b, j: (b * nl + j, 0)),
                  pl.BlockSpec((8, width), lambda b, j: (jnp.maximum((b * nl + j) * (tm // 8) - 1, 0), 0)),
                  vec(mu), vec(w0), vec(a0), vec(k_k), vec(k_a), vec(r_k),
                  vec(wup), vec(aup), vec(gup), vec(bd)],
        out_specs=[out] * 8,
        out_shape=[jax.ShapeDtypeStruct((tp, 512), F32)] * 8,
        compiler_params=_params("parallel", "parallel"),
        name="rwkv_prep",
    )(pr, pr, mu, w0, a0, k_k, k_a, r_k, wup, aup, gup, bd)


def _bdot(a, b):
    return _dot(a.astype(BF16), b.astype(BF16))


def _rwkv_scan_kernel(r_ref, lw_ref, k_ref, v_ref, kn_ref, b_ref, o_ref, z_ref):
    c = CHUNK
    w = 4 * c

    @pl.when(pl.program_id(1) == 0)
    def _():
        z_ref[...] = jnp.zeros_like(z_ref)

    ri = lax.broadcasted_iota(jnp.int32, (w, w), 0)
    ci = lax.broadcasted_iota(jnp.int32, (w, w), 1)
    same_head = (ri >> 6) == (ci >> 6)
    strict = (ri & 63) > (ci & 63)
    incl = (ri & 63) >= (ci & 63)
    eye = ri == ci
    tri = (lax.broadcasted_iota(jnp.int32, (c, c), 0) >= lax.broadcasted_iota(jnp.int32, (c, c), 1)).astype(F32)

    def stack(x):
        return jnp.where(same_head, jnp.concatenate([x, x, x, x], axis=0), 0.0)

    n_seq = lw_ref.shape[0]
    cum_all = [_dot(tri, lw_ref[p], HIGHEST) for p in range(n_seq)]
    groups = range(z_ref.shape[0])
    st = [dict() for _ in groups]
    for q in groups:
        p = q // 2
        lanes = slice(w * (q % 2), w * (q % 2) + w)
        lw = lw_ref[p, :, lanes]
        cum = cum_all[p][:, lanes]
        cum_prev = cum - lw
        mid = cum[c // 2 - 1:c // 2, :]
        tot = cum[c - 1:c, :]
        a = -kn_ref[p, :, lanes]
        b = b_ref[p, :, lanes]
        k = k_ref[p, :, lanes]
        r = r_ref[p, :, lanes]
        dn = jnp.exp(mid - cum)
        de = jnp.exp(tot - cum)
        a_t = stack(a * jnp.exp(cum_prev - mid))
        b_t = stack(b * dn)
        k_t = stack(k * dn)
        r_t = stack(r * jnp.exp(cum - mid))
        st[q].update(seq=p, lanes=lanes, tot=tot, a_0=stack(a * jnp.exp(cum_prev)), r_0=stack(r * jnp.exp(cum)),
                     b_e=stack(b * de), k_e=stack(k * de), vs=stack(v_ref[p, :, lanes]),
                     lhs=jnp.concatenate([a_t, r_t], axis=0).astype(BF16),
                     rhs=jnp.concatenate([b_t, k_t], axis=0).astype(BF16))
    for q in groups:
        inter = _dot_nt(st[q]["lhs"], st[q]["rhs"])
        l_ab = jnp.where(strict, inter[:w, :w], 0.0)
        st[q].update(l_ak=jnp.where(strict, inter[:w, w:], 0.0), m_rb=jnp.where(incl, inter[w:, :w], 0.0),
                     m_rk=jnp.where(incl, inter[w:, w:], 0.0),
                     t_inv=jnp.where(eye, 1.0, 0.0) + l_ab, l_ab=l_ab)
    for q in groups:
        st[q]["power"] = _bdot(st[q]["l_ab"], st[q]["l_ab"])
    for _ in range(int(math.log2(c)) - 2):
        for q in groups:
            both = _bdot(st[q]["power"], jnp.concatenate([st[q]["power"], st[q]["t_inv"]], axis=1))
            st[q]["power"], st[q]["t_inv"] = both[:, :w], st[q]["t_inv"] + both[:, w:]
    for q in groups:
        st[q]["t_inv"] = st[q]["t_inv"] + _bdot(st[q]["power"], st[q]["t_inv"])
    for q in groups:
        st[q]["pq"] = _bdot(st[q]["t_inv"], jnp.concatenate([st[q]["a_0"], st[q]["l_ak"]], axis=1))
    for q in groups:
        st[q]["mix"] = _bdot(jnp.concatenate([st[q]["m_rb"], st[q]["b_e"].T], axis=0), st[q]["pq"])
    for q in groups:
        mix, tot = st[q]["mix"], st[q]["tot"]
        r_hat = st[q]["r_0"] + mix[:w, :w]
        m_o = mix[:w, w:] + st[q]["m_rk"]
        g = jnp.where(eye, jnp.exp(tot), 0.0) + mix[w:, :w]
        m_h = mix[w:, w:] + st[q]["k_e"].T
        lhs = jnp.concatenate([jnp.concatenate([r_hat, m_o], axis=1), jnp.concatenate([g, m_h], axis=1)], axis=0)
        new = _bdot(lhs, jnp.concatenate([z_ref[q], st[q]["vs"]], axis=0))
        o_rows = new[:w]
        o_ref[st[q]["seq"], :, st[q]["lanes"]] = o_rows[0:c] + o_rows[c:2 * c] + o_rows[2 * c:3 * c] + o_rows[3 * c:4 * c]
        z_ref[q] = new[w:]


SCAN_SEQS = 2


def _rwkv_scan(r, lw, k, v, kn, b, batch, lp, s_len):
    nc = s_len // CHUNK + 1
    assert batch % SCAN_SEQS == 0
    pairs = batch // SCAN_SEQS
    ins = [a.reshape(pairs, SCAN_SEQS, lp, 512) for a in (r, lw, k, v, kn, b)]
    blk = pl.BlockSpec((None, SCAN_SEQS, CHUNK, 512), lambda bi, ci: (bi, 0, SCAN_C0 + ci, 0))
    out = pl.pallas_call(
        _rwkv_scan_kernel,
        grid=(pairs, nc),
        in_specs=[blk] * 6,
        out_specs=pl.BlockSpec((None, SCAN_SEQS, CHUNK, 512), lambda bi, ci: (bi, 0, jnp.maximum(ci - 1, 0), 0)),
        out_shape=jax.ShapeDtypeStruct((pairs, SCAN_SEQS, s_len, 512), F32),
        scratch_shapes=[pltpu.VMEM((2 * SCAN_SEQS, 256, 256), F32)],
        compiler_params=_params("parallel", "arbitrary"),
        name="rwkv_scan",
    )(*ins)
    return out.reshape(batch * s_len, 512)


def _outproj_kernel(x_ref, om_ref, orw_ref, bonus_ref, g_ref, og_ref, lnw_ref, lnb_ref, bd_ref,
                    wout_ref, gffn_ref, wq_ref, h2_ref, t_ref, q_ref):
    y_mla = _rms(om_ref[...], og_ref[...])
    o = orw_ref[...]
    bd = bd_ref[...]
    mean = _dot_split(o, bd) * (1.0 / RWKV_HEAD)
    cen = o - mean
    var = _dot_split(cen * cen, bd) * (1.0 / RWKV_HEAD)
    y_rwkv = (cen * lax.rsqrt(var + GN_EPS) * lnw_ref[...] + lnb_ref[...] + bonus_ref[...]) * g_ref[...]
    y = jnp.concatenate([y_mla, y_rwkv], axis=-1).astype(BF16)
    h2 = x_ref[...] + _dot(y, wout_ref[...])
    h2_ref[...] = h2
    t = _rms(h2, gffn_ref[...]).astype(BF16)
    t_ref[...] = t
    q_ref[...] = _dot(t, wq_ref[...])


def _outproj(x, o_mla, o_rwkv, bonus, g, og, lnw, lnb, bd, wout, gffn, wq, batch, lp, s_len, tm):
    t_all, d = x.shape
    ns = s_len // tm
    npad = lp // tm
    xrow = lambda w: pl.BlockSpec((tm, w), lambda b, j: (b * ns + j, 0))
    prow = lambda w: pl.BlockSpec((tm, w), lambda b, j: (b * npad + X0 // tm + j, 0))
    vec = lambda a: _full(a.shape)
    return pl.pallas_call(
        _outproj_kernel,
        grid=(batch, ns),
        in_specs=[xrow(d), xrow(512), xrow(512), prow(512), prow(512), vec(og), vec(lnw), vec(lnb), vec(bd),
                  vec(wout), vec(gffn), vec(wq)],
        out_specs=[xrow(d), xrow(d), xrow(wq.shape[1])],
        out_shape=[jax.ShapeDtypeStruct((t_all, d), F32), jax.ShapeDtypeStruct((t_all, d), BF16),
                   jax.ShapeDtypeStruct((t_all, wq.shape[1]), F32)],
        compiler_params=_params("parallel", "parallel"),
        name="outproj",
    )(x, o_mla, o_rwkv, bonus, g, og, lnw, lnb, bd, wout, gffn, wq)


_CANDS = [(i, j) for i in range(PEER_TOPK) for j in range(PEER_TOPK) if (i + 1) * (j + 1) <= PEER_TOPK]


def _exchange(v, i, l):
    v[i], v[l] = jnp.maximum(v[i], v[l]), jnp.minimum(v[i], v[l])


def _bitonic_merge_desc(v):
    n = len(v)
    j = n // 2
    while j >= 1:
        for i in range(n):
            if i ^ j > i:
                _exchange(v, i, i ^ j)
        j //= 2
    return v


def _bitonic_sort_desc(v):
    n = len(v)
    k = 2
    while k <= n:
        j = k // 2
        while j >= 1:
            for i in range(n):
                l = i ^ j
                if l > i:
                    if (i & k) == 0 or k == n:
                        _exchange(v, i, l)
                    else:
                        _exchange(v, l, i)
            j //= 2
        k *= 2
    return v


def _top16_rows(s):
    v = _bitonic_sort_desc([s[8 * k:8 * k + 8, :] for k in range(PEER_TOPK)])
    for shift in (4, 2, 1):
        other = [pltpu.roll(x, shift, 0) for x in v]
        v = _bitonic_merge_desc([jnp.maximum(v[i], other[PEER_TOPK - 1 - i]) for i in range(PEER_TOPK)])
    return v


def _top_distinct(x, weight, n):
    out = []
    for _ in range(n):
        m = jnp.max(x, axis=0, keepdims=True)
        eq = x == m
        cnt = jnp.sum(jnp.where(eq, weight, 0.0), axis=0, keepdims=True)
        x = jnp.where(eq, -jnp.inf, x)
        out.append((m, cnt))
    return out


def _route_kernel(q_ref, keys_ref, s1_ref, s2_ref, e1_ref, e2_ref, tau_ref, *, heads, tm):
    qb = q_ref[...].astype(BF16)
    pad = 56 - len(_CANDS)
    neg = jnp.full((pad, tm), -jnp.inf, F32)
    cw = jnp.concatenate([jnp.ones((len(_CANDS), tm), F32), jnp.zeros((pad, tm), F32)], axis=0)
    for h in range(heads):
        s = [_dot_nt(keys_ref[2 * h + c], qb[:, 128 * (2 * h + c):128 * (2 * h + c + 1)]) for c in range(2)]
        top = [[x[0:1, :] for x in _top16_rows(s[c])] for c in range(2)]
        cv = jnp.concatenate([top[0][i] + top[1][j] for i, j in _CANDS] + [neg], axis=0)
        tau = jnp.full((1, tm), -jnp.inf, F32)
        seen = jnp.zeros((1, tm), F32)
        for m, cnt in _top_distinct(cv, cw, PEER_TOPK):
            seen = seen + cnt
            tau = jnp.maximum(tau, jnp.where(seen >= PEER_TOPK, m, -jnp.inf))
        m1, m2 = top[0][0], top[1][0]
        z = jnp.sum(jnp.where(cv >= tau, cw * jnp.exp(cv - (m1 + m2)), 0.0), axis=0, keepdims=True)
        e1 = jnp.exp(s[0] - m1) / z
        e2 = jnp.exp(s[1] - m2)
        for lb in range(tm // 128):
            lanes = slice(128 * lb, 128 * lb + 128)
            s1_ref[lb, h] = s[0][:, lanes]
            s2_ref[lb, h] = s[1][:, lanes]
            e1_ref[lb, h] = e1[:, lanes]
            e2_ref[lb, h] = e2[:, lanes]
            tau_ref[lb, h:h + 1, :] = tau[:, lanes]


def _route(q, keys, heads, tm):
    t_all = q.shape[0]
    big = pl.BlockSpec((tm // 128, heads, 128, 128), lambda i: (i, 0, 0, 0))
    return pl.pallas_call(
        functools.partial(_route_kernel, heads=heads, tm=tm),
        grid=(t_all // tm,),
        in_specs=[pl.BlockSpec((tm, q.shape[1]), lambda i: (i, 0)), _full(keys.shape)],
        out_specs=[big] * 4 + [pl.BlockSpec((tm // 128, heads, 128), lambda i: (i, 0, 0))],
        out_shape=[jax.ShapeDtypeStruct((t_all // 128, heads, 128, 128), F32)] * 4
                  + [jax.ShapeDtypeStruct((t_all // 128, heads, 128), F32)],
        compiler_params=_params("parallel"),
        name="peer_route",
    )(q, keys)


GATE_GROUP = 2


def _peer_kernel(t_ref, u_ref, vt_ref, s1_ref, s2_ref, e1_ref, e2_ref, tau_ref, h2_ref, gf_ref,
                 y_ref, acc_ref, gate_ref, hp_ref, act_ref, *, heads, te):
    j = pl.program_id(1)
    tm = t_ref.shape[0]
    half = tm // 2
    n_a = te // 128
    n_g = n_a // GATE_GROUP
    rows = 128 * GATE_GROUP
    tile = (8, 128)
    assert n_a == 8 and half % 128 == 0

    @pl.when(j == 0)
    def _():
        acc_ref[...] = jnp.zeros_like(acc_ref)

    a_rows = pl.ds(pl.multiple_of(j * n_a, 8), 8)

    def gate_group(lb, g):
        acc = [[jnp.zeros(tile, F32) for _ in range(16)] for _ in range(GATE_GROUP)]
        for h in range(heads):
            tau = jnp.broadcast_to(tau_ref[lb, h:h + 1, :], tile)
            s1 = s1_ref[lb, h, a_rows, :]
            e1 = e1_ref[lb, h, a_rows, :]
            s1b = [jnp.broadcast_to(s1[g * GATE_GROUP + ai:g * GATE_GROUP + ai + 1, :], tile) for ai in range(GATE_GROUP)]
            e1b = [jnp.broadcast_to(e1[g * GATE_GROUP + ai:g * GATE_GROUP + ai + 1, :], tile) for ai in range(GATE_GROUP)]
            for r in range(16):
                s2 = s2_ref[lb, h, 8 * r:8 * r + 8, :]
                e2 = e2_ref[lb, h, 8 * r:8 * r + 8, :]
                for ai in range(GATE_GROUP):
                    acc[ai][r] = acc[ai][r] + jnp.where(s2 + s1b[ai] >= tau, e2, 0.0) * e1b[ai]
        for ai in range(GATE_GROUP):
            for r in range(16):
                row = (g * GATE_GROUP + ai) * 128 + 8 * r
                gate_ref[lb, row:row + 8, :] = acc[ai][r]

    def activate(lbs):
        for lb in lbs:
            lanes = slice(128 * lb, 128 * lb + 128)
            x = hp_ref[:, lanes]
            act_ref[:, lanes] = (0.5 * x * (1.0 + lax.erf(x * (1.0 / math.sqrt(2.0)))) * gate_ref[lb]).astype(BF16)

    def second(g, lanes):
        piece = slice(rows * g, rows * (g + 1))
        acc_ref[:, lanes] += _dot(vt_ref[:, piece], act_ref[piece, lanes])

    lbs_a = range(0, half // 128)
    lbs_b = range(half // 128, tm // 128)
    tokens = t_ref[...]
    for g in range(n_g):
        piece = slice(rows * g, rows * (g + 1))
        hp_ref[piece, :] = _dot_nt(u_ref[piece, :], tokens)
        for lb in lbs_a:
            gate_group(lb, g)
    activate(lbs_a)
    for g in range(n_g):
        second(g, slice(0, half))
        for lb in lbs_b:
            gate_group(lb, g)
    activate(lbs_b)
    for g in range(n_g):
        second(g, slice(half, tm))

    @pl.when(j == pl.num_programs(1) - 1)
    def _():
        y_ref[...] = _rms(h2_ref[...] + acc_ref[...].T, gf_ref[...])


def _peer(t, u, vt, s1, s2, e1, e2, tau, h2, gf, heads, tm, te):
    t_all, d = t.shape
    n_exp = u.shape[0]
    tok = pl.BlockSpec((tm // 128, heads, 128, 128), lambda i, j: (i, 0, 0, 0))
    return pl.pallas_call(
        functools.partial(_peer_kernel, heads=heads, te=te),
        grid=(t_all // tm, n_exp // te),
        in_specs=[pl.BlockSpec((tm, d), lambda i, j: (i, 0)),
                  pl.BlockSpec((te, d), lambda i, j: (j, 0)),
                  pl.BlockSpec((d, te), lambda i, j: (0, j)),
                  tok, tok, tok, tok,
                  pl.BlockSpec((tm // 128, heads, 128), lambda i, j: (i, 0, 0)),
                  pl.BlockSpec((tm, d), lambda i, j: (i, 0)),
                  _full(gf.shape)],
        out_specs=pl.BlockSpec((tm, d), lambda i, j: (i, 0)),
        out_shape=jax.ShapeDtypeStruct((t_all, d), F32),
        scratch_shapes=[pltpu.VMEM((d, tm), F32), pltpu.VMEM((tm // 128, te, 128), F32),
                        pltpu.VMEM((te, tm), F32), pltpu.VMEM((te, tm), BF16)],
        compiler_params=_params("parallel", "arbitrary"),
        name="peer_experts",
    )(t, u, vt, s1, s2, e1, e2, tau, h2, gf)


def _tile(n, cap):
    t = cap
    while n % t:
        t //= 2
    return t


def kernel(x, meta_tokens, norm_mix_g, w_in, mla_q_norm_g, mla_w_uq, mla_kv_norm_g, mla_w_ukv, mla_out_g, rwkv_mu, rwkv_w0, rwkv_w_up, rwkv_a0, rwkv_a_up, rwkv_g_up, rwkv_k_k, rwkv_k_a, rwkv_r_k, rwkv_ln_w, rwkv_ln_b, w_out, norm_ffn_g, peer_w_q, peer_sub_keys, peer_u, peer_v, norm_final_g):
    batch, s_len, d = x.shape
    assert w_in.shape[0] == 1, "one layer"
    assert s_len % ATT_Q == 0 and d == 1024
    lp = X0 + s_len
    tp = batch * lp
    row = lambda a: a.reshape(1, -1).astype(F32)

    h = jnp.concatenate([jnp.zeros((batch, META0, d), F32),
                         jnp.broadcast_to(meta_tokens[None].astype(F32), (batch, N_META, d)), x], axis=1)
    h = h.reshape(tp, d)

    wi = w_in[0]
    kr = wi[:, 384:448]
    z64 = jnp.zeros((d, 64), F32)
    w_ext = jnp.concatenate([wi[:, 0:384], kr, z64, kr[:, 32:], kr[:, :32], z64, wi[:, 448:]], axis=1).astype(BF16)
    n_mla = 640
    pm, pr = _inproj(h, row(norm_mix_g[0]), w_ext, n_mla, _tile(tp, 512))

    pos = jnp.maximum(jnp.arange(lp) - META0, 0).astype(F32)
    inv = ROPE_THETA ** (-jnp.arange(32, dtype=F32) / 32)
    ang = pos[:, None] * inv[None, :]
    zl = jnp.zeros((lp, 64), F32)
    cos = jnp.tile(jnp.concatenate([jnp.cos(ang), jnp.cos(ang), zl], axis=1), (batch, 1))
    sin = jnp.tile(jnp.concatenate([-jnp.sin(ang), jnp.sin(ang), zl], axis=1), (batch, 1))

    wuq = mla_w_uq[0].reshape(256, MLA_HEADS, 192)
    zq = jnp.zeros((256, MLA_HEADS, 64), F32)
    wqa = jnp.concatenate([wuq, zq], axis=2).reshape(256, MLA_HEADS * 256).astype(BF16)
    wqb = jnp.concatenate([wuq[:, :, 160:192], wuq[:, :, 128:160], zq], axis=2).reshape(256, MLA_HEADS * 128).astype(BF16)
    wukv = mla_w_ukv[0].reshape(128, MLA_HEADS, 256)
    wk = wukv[:, :, :128].reshape(128, MLA_HEADS * 128).astype(BF16)
    wv = wukv[:, :, 128:].reshape(128, MLA_HEADS * 128).astype(BF16)
    q, k, v = _mla_prep(pm, cos, sin, row(mla_q_norm_g[0]), row(mla_kv_norm_g[0]), wqa, wqb, wk, wv,
                        192 ** -0.5, _tile(tp, 512))
    o_mla = _attention(q, k, v, batch, lp, s_len)

    z_up = jnp.zeros((64, 512), F32)
    wup = jnp.concatenate([rwkv_w_up[0], z_up], axis=0).astype(BF16)
    aup = jnp.concatenate([z_up, rwkv_a_up[0]], axis=0).astype(BF16)
    head_id = jnp.arange(512) // RWKV_HEAD
    bd = (head_id[:, None] == head_id[None, :]).astype(BF16)
    r, lw, k2, vv, kn, bv, bonus, g = _rwkv_prep(
        pr, row(rwkv_mu[0]), row(rwkv_w0[0]), row(rwkv_a0[0]), row(rwkv_k_k[0]), row(rwkv_k_a[0]),
        row(rwkv_r_k[0]), wup, aup, rwkv_g_up[0].astype(BF16), bd, batch, lp, 256)
    o_rwkv = _rwkv_scan(r, lw, k2, vv, kn, bv, batch, lp, s_len)

    heads = peer_sub_keys.shape[1]
    h2, t, pq = _outproj(x.reshape(batch * s_len, d), o_mla, o_rwkv, bonus, g, row(mla_out_g[0]),
                         row(rwkv_ln_w[0]), row(rwkv_ln_b[0]), bd, w_out[0].astype(BF16), row(norm_ffn_g[0]),
                         peer_w_q[0].astype(BF16), batch, lp, s_len, _tile(s_len, 512))

    t_all = batch * s_len
    keys = peer_sub_keys[0].reshape(heads * 2, 128, 128).astype(BF16)
    tm = _tile(t_all, 512)
    s1, s2, e1, e2, tau = _route(pq, keys, heads, tm)
    y = _peer(t, peer_u[0].astype(BF16), peer_v[0].T.astype(BF16), s1, s2, e1, e2, tau, h2,
              row(norm_final_g), heads, tm, 1024)
    return y.reshape(batch, s_len, d)
```

```python
import functools
import math

import jax
import jax.numpy as jnp
from jax import lax
from jax.experimental import pallas as pl
from jax.experimental.pallas import tpu as pltpu

F32 = jnp.float32
BF16 = jnp.bfloat16
HIGHEST = lax.Precision.HIGHEST

RMS_EPS = 1e-6
GN_EPS = 64e-5
ROPE_THETA = 10000.0
N_META = 16
CHUNK = 64
X0 = 512
META0 = X0 - N_META
K0 = X0 - 128
SCAN_C0 = (X0 - CHUNK) // CHUNK
MLA_HEADS = 4
RWKV_HEAD = 64
PEER_TOPK = 16
VMEM_LIMIT = 56 * 1024 * 1024


def _dot(a, b, precision=None):
    return jnp.dot(a, b, preferred_element_type=F32, precision=precision)


def _dot_nt(a, b, precision=None):
    return lax.dot_general(a, b, (((1,), (1,)), ((), ())),
                           preferred_element_type=F32, precision=precision)


def _dot_split(x, m):
    hi = x.astype(BF16)
    lo = (x - hi.astype(F32)).astype(BF16)
    return _dot(hi, m) + _dot(lo, m)


def _rms(x, g):
    return x * lax.rsqrt(jnp.mean(x * x, axis=-1, keepdims=True) + RMS_EPS) * g


def _params(*sem):
    return pltpu.CompilerParams(dimension_semantics=sem, vmem_limit_bytes=VMEM_LIMIT)


def _full(shape):
    n = len(shape)
    return pl.BlockSpec(shape, lambda *_: (0,) * n)


def _inproj_kernel(h_ref, g_ref, w_ref, pm_ref, pr_ref, *, n_mla):
    n = _rms(h_ref[...], g_ref[...]).astype(BF16)
    p = _dot(n, w_ref[...])
    pm_ref[...] = p[:, :n_mla]
    pr_ref[...] = p[:, n_mla:]


def _inproj(h, g, w, n_mla, tm):
    tp, d = h.shape
    n_all = w.shape[1]
    return pl.pallas_call(
        functools.partial(_inproj_kernel, n_mla=n_mla),
        grid=(tp // tm,),
        in_specs=[pl.BlockSpec((tm, d), lambda i: (i, 0)), _full((1, d)), _full(w.shape)],
        out_specs=[pl.BlockSpec((tm, n_mla), lambda i: (i, 0)),
                   pl.BlockSpec((tm, n_all - n_mla), lambda i: (i, 0))],
        out_shape=[jax.ShapeDtypeStruct((tp, n_mla), F32),
                   jax.ShapeDtypeStruct((tp, n_all - n_mla), F32)],
        compiler_params=_params("parallel"),
        name="inproj",
    )(h, g, w)


def _mla_prep_kernel(pm_ref, cos_ref, sin_ref, gq_ref, gkv_ref, wqa_ref, wqb_ref, wk_ref, wv_ref,
                     q_ref, k_ref, v_ref, *, scale):
    pm = pm_ref[...]
    nq = _rms(pm[:, 0:256], gq_ref[...]).astype(BF16)
    nkv = _rms(pm[:, 256:384], gkv_ref[...]).astype(BF16)
    cos, sin = cos_ref[...], sin_ref[...]
    qa = _dot(nq, wqa_ref[...])
    qb = _dot(nq, wqb_ref[...])
    kn = _dot(nkv, wk_ref[...])
    vv = _dot(nkv, wv_ref[...])
    kr = pm[:, 384:512] * cos + pm[:, 512:640] * sin
    for h in range(MLA_HEADS):
        qn = qa[:, 256 * h:256 * h + 128]
        qr = qa[:, 256 * h + 128:256 * h + 256] * cos + qb[:, 128 * h:128 * h + 128] * sin
        q_ref[h] = (jnp.concatenate([qn, qr], axis=-1) * scale).astype(BF16)
        k_ref[h] = jnp.concatenate([kn[:, 128 * h:128 * h + 128], kr], axis=-1).astype(BF16)
        v_ref[h] = vv[:, 128 * h:128 * h + 128].astype(BF16)


def _mla_prep(pm, cos, sin, gq, gkv, wqa, wqb, wk, wv, scale, tm):
    tp = pm.shape[0]
    row = lambda w: pl.BlockSpec((tm, w), lambda i: (i, 0))
    head = lambda w: pl.BlockSpec((MLA_HEADS, tm, w), lambda i: (0, i, 0))
    return pl.pallas_call(
        functools.partial(_mla_prep_kernel, scale=scale),
        grid=(tp // tm,),
        in_specs=[row(pm.shape[1]), row(128), row(128), _full(gq.shape), _full(gkv.shape),
                  _full(wqa.shape), _full(wqb.shape), _full(wk.shape), _full(wv.shape)],
        out_specs=[head(256), head(256), head(128)],
        out_shape=[jax.ShapeDtypeStruct((MLA_HEADS, tp, 256), BF16),
                   jax.ShapeDtypeStruct((MLA_HEADS, tp, 256), BF16),
                   jax.ShapeDtypeStruct((MLA_HEADS, tp, 128), BF16)],
        compiler_params=_params("parallel"),
        name="mla_prep",
    )(pm, cos, sin, gq, gkv, wqa, wqb, wk, wv)


ATT_Q = 256


def _attn_kernel(q_ref, k_ref, v_ref, o_ref, *, n_q):
    for qi in range(n_q):
        q0 = X0 + ATT_Q * qi
        kend = q0 + ATT_Q
        n = kend - K0
        q = q_ref[q0:q0 + ATT_Q, :]
        s = _dot_nt(q, k_ref[K0:kend, :])
        kk = lax.broadcasted_iota(jnp.int32, (1, n), 1)
        ck = jnp.where(kk < 128 - N_META, 1 << 20, kk >> 6)
        cq = ((ATT_Q * qi + lax.broadcasted_iota(jnp.int32, (ATT_Q, 1), 0)) >> 6) + 2
        s = jnp.where(ck <= cq, s, -1e30)
        e = jnp.exp(s - jnp.max(s, axis=-1, keepdims=True))
        l = jnp.sum(e, axis=-1, keepdims=True)
        o = _dot(e.astype(BF16), v_ref[K0:kend, :])
        o_ref[ATT_Q * qi:ATT_Q * (qi + 1), :] = o / l


def _attention(q, k, v, batch, lp, s_len):
    return pl.pallas_call(
        functools.partial(_attn_kernel, n_q=s_len // ATT_Q),
        grid=(batch, MLA_HEADS),
        in_specs=[pl.BlockSpec((None, lp, 256), lambda b, h: (h, b, 0)),
                  pl.BlockSpec((None, lp, 256), lambda b, h: (h, b, 0)),
                  pl.BlockSpec((None, lp, 128), lambda b, h: (h, b, 0))],
        out_specs=pl.BlockSpec((s_len, 128), lambda b, h: (b, h)),
        out_shape=jax.ShapeDtypeStruct((batch * s_len, MLA_HEADS * 128), F32),
        compiler_params=_params("parallel", "parallel"),
        name="mla_attn",
    )(q, k, v)


def _softplus(x):
    return jnp.maximum(x, 0.0) + jnp.log(1.0 + jnp.exp(-jnp.abs(x)))


def _sigmoid(x):
    return 1.0 / (1.0 + jnp.exp(-x))


def _rwkv_prep_kernel(p_ref, prev_ref, mu_ref, w0_ref, a0_ref, kk_ref, ka_ref, rk_ref,
                      wup_ref, aup_ref, gup_ref, bd_ref,
                      r_ref, lw_ref, k_ref, v_ref, kn_ref, b_ref, bonus_ref, g_ref, *, tm):
    p = p_ref[...]
    rows = lax.broadcasted_iota(jnp.int32, (tm, 1), 0)
    shifted = jnp.where(rows == 0, prev_ref[7:8, :], pltpu.roll(p, 1, 0))
    pm = p + mu_ref[...] * (shifted - p)
    r, k, v = pm[:, 0:512], pm[:, 512:1024], pm[:, 1024:1536]
    dd, dg = pm[:, 1536:1664], pm[:, 1664:1792]
    w = -_softplus(-(w0_ref[...] + _dot(jnp.tanh(dd).astype(BF16), wup_ref[...]))) - 0.5
    log_decay = -jnp.exp(w)
    a = _sigmoid(a0_ref[...] + _dot(dd.astype(BF16), aup_ref[...]))
    g_ref[...] = _dot(_sigmoid(dg).astype(BF16), gup_ref[...])
    bd = bd_ref[...]
    kk = k * kk_ref[...]
    norm = jnp.sqrt(_dot_split(kk * kk, bd))
    kk = kk / jnp.maximum(norm, 1e-12)
    k2 = k * (1.0 + (a - 1.0) * ka_ref[...])
    real = (pl.program_id(1) * tm + rows) >= META0
    r_ref[...] = r
    lw_ref[...] = jnp.where(real, log_decay, 0.0)
    k_ref[...] = k2
    v_ref[...] = v
    kn_ref[...] = kk
    b_ref[...] = kk * a
    bonus_ref[...] = _dot_split(r * k2 * rk_ref[...], bd) * v


def _rwkv_prep(pr, mu, w0, a0, k_k, k_a, r_k, wup, aup, gup, bd, batch, lp, tm):
    tp, width = pr.shape
    nl = lp // tm
    vec = lambda a: _full(a.shape)
    out = pl.BlockSpec((tm, 512), lambda b, j: (b * nl + j, 0))
    return pl.pallas_call(
        functools.partial(_rwkv_prep_kernel, tm=tm),
        grid=(batch, nl),
        in_specs=[pl.BlockSpec((tm, width), lambda b, j: (b * nl + j, 0)),
                  pl.BlockSpec((8, width), lambda b, j: (jnp.maximum((b * nl + j) * (tm // 8) - 1, 0), 0)),
                  vec(mu), vec(w0), vec(a0), vec(k_k), vec(k_a), vec(r_k),
                  vec(wup), vec(aup), vec(gup), vec(bd)],
        out_specs=[out] * 8,
        out_shape=[jax.ShapeDtypeStruct((tp, 512), F32)] * 8,
        compiler_params=_params("parallel", "parallel"),
        name="rwkv_prep",
    )(pr, pr, mu, w0, a0, k_k, k_a, r_k, wup, aup, gup, bd)


def _bdot(a, b):
    return _dot(a.astype(BF16), b.astype(BF16))


def _rwkv_scan_kernel(r_ref, lw_ref, k_ref, v_ref, kn_ref, b_ref, o_ref, z_ref):
    c = CHUNK
    w = 4 * c

    @pl.when(pl.program_id(1) == 0)
    def _():
        z_ref[...] = jnp.zeros_like(z_ref)

    ri = lax.broadcasted_iota(jnp.int32, (w, w), 0)
    ci = lax.broadcasted_iota(jnp.int32, (w, w), 1)
    same_head = (ri >> 6) == (ci >> 6)
    strict = (ri & 63) > (ci & 63)
    incl = (ri & 63) >= (ci & 63)
    eye = ri == ci
    tri = (lax.broadcasted_iota(jnp.int32, (c, c), 0) >= lax.broadcasted_iota(jnp.int32, (c, c), 1)).astype(F32)

    def stack(x):
        return jnp.where(same_head, jnp.concatenate([x, x, x, x], axis=0), 0.0)

    n_seq = lw_ref.shape[0]
    cum_all = [_dot(tri, lw_ref[p], HIGHEST) for p in range(n_seq)]
    groups = range(z_ref.shape[0])
    st = [dict() for _ in groups]
    for q in groups:
        p = q // 2
        lanes = slice(w * (q % 2), w * (q % 2) + w)
        lw = lw_ref[p, :, lanes]
        cum = cum_all[p][:, lanes]
        cum_prev = cum - lw
        mid = cum[c // 2 - 1:c // 2, :]
        tot = cum[c - 1:c, :]
        a = -kn_ref[p, :, lanes]
        b = b_ref[p, :, lanes]
        k = k_ref[p, :, lanes]
        r = r_ref[p, :, lanes]
        dn = jnp.exp(mid - cum)
        de = jnp.exp(tot - cum)
        a_t = stack(a * jnp.exp(cum_prev - mid))
        b_t = stack(b * dn)
        k_t = stack(k * dn)
        r_t = stack(r * jnp.exp(cum - mid))
        st[q].update(seq=p, lanes=lanes, tot=tot, a_0=stack(a * jnp.exp(cum_prev)), r_0=stack(r * jnp.exp(cum)),
                     b_e=stack(b * de), k_e=stack(k * de), vs=stack(v_ref[p, :, lanes]),
                     lhs=jnp.concatenate([a_t, r_t], axis=0).astype(BF16),
                     rhs=jnp.concatenate([b_t, k_t], axis=0).astype(BF16))
    for q in groups:
        inter = _dot_nt(st[q]["lhs"], st[q]["rhs"])
        l_ab = jnp.where(strict, inter[:w, :w], 0.0)
        st[q].update(l_ak=jnp.where(strict, inter[:w, w:], 0.0), m_rb=jnp.where(incl, inter[w:, :w], 0.0),
                     m_rk=jnp.where(incl, inter[w:, w:], 0.0),
                     t_inv=jnp.where(eye, 1.0, 0.0) + l_ab, l_ab=l_ab)
    for q in groups:
        st[q]["power"] = _bdot(st[q]["l_ab"], st[q]["l_ab"])
    for _ in range(int(math.log2(c)) - 2):
        for q in groups:
            both = _bdot(st[q]["power"], jnp.concatenate([st[q]["power"], st[q]["t_inv"]], axis=1))
            st[q]["power"], st[q]["t_inv"] = both[:, :w], st[q]["t_inv"] + both[:, w:]
    for q in groups:
        st[q]["t_inv"] = st[q]["t_inv"] + _bdot(st[q]["power"], st[q]["t_inv"])
    for q in groups:
        st[q]["pq"] = _bdot(st[q]["t_inv"], jnp.concatenate([st[q]["a_0"], st[q]["l_ak"]], axis=1))
    for q in groups:
        st[q]["mix"] = _bdot(jnp.concatenate([st[q]["m_rb"], st[q]["b_e"].T], axis=0), st[q]["pq"])
    for q in groups:
        mix, tot = st[q]["mix"], st[q]["tot"]
        r_hat = st[q]["r_0"] + mix[:w, :w]
        m_o = mix[:w, w:] + st[q]["m_rk"]
        g = jnp.where(eye, jnp.exp(tot), 0.0) + mix[w:, :w]
        m_h = mix[w:, w:] + st[q]["k_e"].T
        lhs = jnp.concatenate([jnp.concatenate([r_hat, m_o], axis=1), jnp.concatenate([g, m_h], axis=1)], axis=0)
        new = _bdot(lhs, jnp.concatenate([z_ref[q], st[q]["vs"]], axis=0))
        o_rows = new[:w]
        o_ref[st[q]["seq"], :, st[q]["lanes"]] = o_rows[0:c] + o_rows[c:2 * c] + o_rows[2 * c:3 * c] + o_rows[3 * c:4 * c]
        z_ref[q] = new[w:]


SCAN_SEQS = 2


def _rwkv_scan(r, lw, k, v, kn, b, batch, lp, s_len):
    nc = s_len // CHUNK + 1
    assert batch % SCAN_SEQS == 0
    pairs = batch // SCAN_SEQS
    ins = [a.reshape(pairs, SCAN_SEQS, lp, 512) for a in (r, lw, k, v, kn, b)]
    blk = pl.BlockSpec((None, SCAN_SEQS, CHUNK, 512), lambda bi, ci: (bi, 0, SCAN_C0 + ci, 0))
    out = pl.pallas_call(
        _rwkv_scan_kernel,
        grid=(pairs, nc),
        in_specs=[blk] * 6,
        out_specs=pl.BlockSpec((None, SCAN_SEQS, CHUNK, 512), lambda bi, ci: (bi, 0, jnp.maximum(ci - 1, 0), 0)),
        out_shape=jax.ShapeDtypeStruct((pairs, SCAN_SEQS, s_len, 512), F32),
        scratch_shapes=[pltpu.VMEM((2 * SCAN_SEQS, 256, 256), F32)],
        compiler_params=_params("parallel", "arbitrary"),
        name="rwkv_scan",
    )(*ins)
    return out.reshape(batch * s_len, 512)


def _outproj_kernel(x_ref, om_ref, orw_ref, bonus_ref, g_ref, og_ref, lnw_ref, lnb_ref, bd_ref,
                    wout_ref, gffn_ref, wq_ref, h2_ref, t_ref, q_ref):
    y_mla = _rms(om_ref[...], og_ref[...])
    o = orw_ref[...]
    bd = bd_ref[...]
    mean = _dot_split(o, bd) * (1.0 / RWKV_HEAD)
    cen = o - mean
    var = _dot_split(cen * cen, bd) * (1.0 / RWKV_HEAD)
    y_rwkv = (cen * lax.rsqrt(var + GN_EPS) * lnw_ref[...] + lnb_ref[...] + bonus_ref[...]) * g_ref[...]
    y = jnp.concatenate([y_mla, y_rwkv], axis=-1).astype(BF16)
    h2 = x_ref[...] + _dot(y, wout_ref[...])
    h2_ref[...] = h2
    t = _rms(h2, gffn_ref[...]).astype(BF16)
    t_ref[...] = t
    q_ref[...] = _dot(t, wq_ref[...])


def _outproj(x, o_mla, o_rwkv, bonus, g, og, lnw, lnb, bd, wout, gffn, wq, batch, lp, s_len, tm):
    t_all, d = x.shape
    ns = s_len // tm
    npad = lp // tm
    xrow = lambda w: pl.BlockSpec((tm, w), lambda b, j: (b * ns + j, 0))
    prow = lambda w: pl.BlockSpec((tm, w), lambda b, j: (b * npad + X0 // tm + j, 0))
    vec = lambda a: _full(a.shape)
    return pl.pallas_call(
        _outproj_kernel,
        grid=(batch, ns),
        in_specs=[xrow(d), xrow(512), xrow(512), prow(512), prow(512), vec(og), vec(lnw), vec(lnb), vec(bd),
                  vec(wout), vec(gffn), vec(wq)],
        out_specs=[xrow(d), xrow(d), xrow(wq.shape[1])],
        out_shape=[jax.ShapeDtypeStruct((t_all, d), F32), jax.ShapeDtypeStruct((t_all, d), BF16),
                   jax.ShapeDtypeStruct((t_all, wq.shape[1]), F32)],
        compiler_params=_params("parallel", "parallel"),
        name="outproj",
    )(x, o_mla, o_rwkv, bonus, g, og, lnw, lnb, bd, wout, gffn, wq)


_CANDS = [(i, j) for i in range(PEER_TOPK) for j in range(PEER_TOPK) if (i + 1) * (j + 1) <= PEER_TOPK]


def _exchange(v, i, l):
    v[i], v[l] = jnp.maximum(v[i], v[l]), jnp.minimum(v[i], v[l])


def _bitonic_merge_desc(v):
    n = len(v)
    j = n // 2
    while j >= 1:
        for i in range(n):
            if i ^ j > i:
                _exchange(v, i, i ^ j)
        j //= 2
    return v


def _bitonic_sort_desc(v):
    n = len(v)
    k = 2
    while k <= n:
        j = k // 2
        while j >= 1:
            for i in range(n):
                l = i ^ j
                if l > i:
                    if (i & k) == 0 or k == n:
                        _exchange(v, i, l)
                    else:
                        _exchange(v, l, i)
            j //= 2
        k *= 2
    return v


def _top16_rows(s):
    v = _bitonic_sort_desc([s[8 * k:8 * k + 8, :] for k in range(PEER_TOPK)])
    for shift in (4, 2, 1):
        other = [pltpu.roll(x, shift, 0) for x in v]
        v = _bitonic_merge_desc([jnp.maximum(v[i], other[PEER_TOPK - 1 - i]) for i in range(PEER_TOPK)])
    return v


def _top_distinct(x, weight, n):
    out = []
    for _ in range(n):
        m = jnp.max(x, axis=0, keepdims=True)
        eq = x == m
        cnt = jnp.sum(jnp.where(eq, weight, 0.0), axis=0, keepdims=True)
        x = jnp.where(eq, -jnp.inf, x)
        out.append((m, cnt))
    return out


def _route_kernel(q_ref, keys_ref, s1_ref, s2_ref, e1_ref, e2_ref, tau_ref, *, heads, tm):
    qb = q_ref[...].astype(BF16)
    pad = 56 - len(_CANDS)
    neg = jnp.full((pad, tm), -jnp.inf, F32)
    cw = jnp.concatenate([jnp.ones((len(_CANDS), tm), F32), jnp.zeros((pad, tm), F32)], axis=0)
    for h in range(heads):
        s = [_dot_nt(keys_ref[2 * h + c], qb[:, 128 * (2 * h + c):128 * (2 * h + c + 1)]) for c in range(2)]
        top = [[x[0:1, :] for x in _top16_rows(s[c])] for c in range(2)]
        cv = jnp.concatenate([top[0][i] + top[1][j] for i, j in _CANDS] + [neg], axis=0)
        tau = jnp.full((1, tm), -jnp.inf, F32)
        seen = jnp.zeros((1, tm), F32)
        for m, cnt in _top_distinct(cv, cw, PEER_TOPK):
            seen = seen + cnt
            tau = jnp.maximum(tau, jnp.where(seen >= PEER_TOPK, m, -jnp.inf))
        m1, m2 = top[0][0], top[1][0]
        z = jnp.sum(jnp.where(cv >= tau, cw * jnp.exp(cv - (m1 + m2)), 0.0), axis=0, keepdims=True)
        e1 = jnp.exp(s[0] - m1) / z
        e2 = jnp.exp(s[1] - m2)
        for lb in range(tm // 128):
            lanes = slice(128 * lb, 128 * lb + 128)
            s1_ref[lb, h] = s[0][:, lanes]
            s2_ref[lb, h] = s[1][:, lanes]
            e1_ref[lb, h] = e1[:, lanes]
            e2_ref[lb, h] = e2[:, lanes]
            tau_ref[lb, h:h + 1, :] = tau[:, lanes]


def _route(q, keys, heads, tm):
    t_all = q.shape[0]
    big = pl.BlockSpec((tm // 128, heads, 128, 128), lambda i: (i, 0, 0, 0))
    return pl.pallas_call(
        functools.partial(_route_kernel, heads=heads, tm=tm),
        grid=(t_all // tm,),
        in_specs=[pl.BlockSpec((tm, q.shape[1]), lambda i: (i, 0)), _full(keys.shape)],
        out_specs=[big] * 4 + [pl.BlockSpec((tm // 128, heads, 128), lambda i: (i, 0, 0))],
        out_shape=[jax.ShapeDtypeStruct((t_all // 128, heads, 128, 128), F32)] * 4
                  + [jax.ShapeDtypeStruct((t_all // 128, heads, 128), F32)],
        compiler_params=_params("parallel"),
        name="peer_route",
    )(q, keys)


GATE_GROUP = 2


def _peer_kernel(t_ref, u_ref, vt_ref, s1_ref, s2_ref, e1_ref, e2_ref, tau_ref, h2_ref, gf_ref,
                 y_ref, acc_ref, gate_ref, *, heads, te):
    j = pl.program_id(1)
    tm = t_ref.shape[0]

    @pl.when(j == 0)
    def _():
        acc_ref[...] = jnp.zeros_like(acc_ref)

    tile = (8, 128)
    n_a = te // 128
    for a8 in range(0, n_a, 8):
        a_rows = pl.ds(pl.multiple_of(j * n_a + a8, 8), 8)
        for ag in range(0, 8, GATE_GROUP):

            def lane_block(lb, carry, a_rows=a_rows, ag=ag, a8=a8):
                acc = [[jnp.zeros(tile, F32) for _ in range(16)] for _ in range(GATE_GROUP)]
                for h in range(heads):
                    tau = jnp.broadcast_to(tau_ref[lb, h:h + 1, :], tile)
                    s1 = s1_ref[lb, h, a_rows, :]
                    e1 = e1_ref[lb, h, a_rows, :]
                    s1b = [jnp.broadcast_to(s1[ag + ai:ag + ai + 1, :], tile) for ai in range(GATE_GROUP)]
                    e1b = [jnp.broadcast_to(e1[ag + ai:ag + ai + 1, :], tile) for ai in range(GATE_GROUP)]
                    for r in range(16):
                        s2 = s2_ref[lb, h, 8 * r:8 * r + 8, :]
                        e2 = e2_ref[lb, h, 8 * r:8 * r + 8, :]
                        for ai in range(GATE_GROUP):
                            acc[ai][r] = acc[ai][r] + jnp.where(s2 + s1b[ai] >= tau, e2, 0.0) * e1b[ai]
                for ai in range(GATE_GROUP):
                    for r in range(16):
                        row = (a8 + ag + ai) * 128 + 8 * r
                        gate_ref[lb, row:row + 8, :] = acc[ai][r]
                return carry

            lax.fori_loop(0, tm // 128, lane_block, 0)

    hp = _dot_nt(u_ref[...], t_ref[...])
    gate = jnp.concatenate([gate_ref[lb] for lb in range(tm // 128)], axis=1)
    act = 0.5 * hp * (1.0 + lax.erf(hp * (1.0 / math.sqrt(2.0)))) * gate
    acc_ref[...] += _dot(vt_ref[...], act.astype(BF16))

    @pl.when(j == pl.num_programs(1) - 1)
    def _():
        y_ref[...] = _rms(h2_ref[...] + acc_ref[...].T, gf_ref[...])


def _peer(t, u, vt, s1, s2, e1, e2, tau, h2, gf, heads, tm, te):
    t_all, d = t.shape
    n_exp = u.shape[0]
    tok = pl.BlockSpec((tm // 128, heads, 128, 128), lambda i, j: (i, 0, 0, 0))
    return pl.pallas_call(
        functools.partial(_peer_kernel, heads=heads, te=te),
        grid=(t_all // tm, n_exp // te),
        in_specs=[pl.BlockSpec((tm, d), lambda i, j: (i, 0)),
                  pl.BlockSpec((te, d), lambda i, j: (j, 0)),
                  pl.BlockSpec((d, te), lambda i, j: (0, j)),
                  tok, tok, tok, tok,
                  pl.BlockSpec((tm // 128, heads, 128), lambda i, j: (i, 0, 0)),
                  pl.BlockSpec((tm, d), lambda i, j: (i, 0)),
                  _full(gf.shape)],
        out_specs=pl.BlockSpec((tm, d), lambda i, j: (i, 0)),
        out_shape=jax.ShapeDtypeStruct((t_all, d), F32),
        scratch_shapes=[pltpu.VMEM((d, tm), F32), pltpu.VMEM((tm // 128, te, 128), F32)],
        compiler_params=_params("parallel", "arbitrary"),
        name="peer_experts",
    )(t, u, vt, s1, s2, e1, e2, tau, h2, gf)


def _tile(n, cap):
    t = cap
    while n % t:
        t //= 2
    return t


def kernel(x, meta_tokens, norm_mix_g, w_in, mla_q_norm_g, mla_w_uq, mla_kv_norm_g, mla_w_ukv, mla_out_g, rwkv_mu, rwkv_w0, rwkv_w_up, rwkv_a0, rwkv_a_up, rwkv_g_up, rwkv_k_k, rwkv_k_a, rwkv_r_k, rwkv_ln_w, rwkv_ln_b, w_out, norm_ffn_g, peer_w_q, peer_sub_keys, peer_u, peer_v, norm_final_g):
    batch, s_len, d = x.shape
    assert w_in.shape[0] == 1, "one layer"
    assert s_len % ATT_Q == 0 and d == 1024
    lp = X0 + s_len
    tp = batch * lp
    row = lambda a: a.reshape(1, -1).astype(F32)

    h = jnp.concatenate([jnp.zeros((batch, META0, d), F32),
                         jnp.broadcast_to(meta_tokens[None].astype(F32), (batch, N_META, d)), x], axis=1)
    h = h.reshape(tp, d)

    wi = w_in[0]
    kr = wi[:, 384:448]
    z64 = jnp.zeros((d, 64), F32)
    w_ext = jnp.concatenate([wi[:, 0:384], kr, z64, kr[:, 32:], kr[:, :32], z64, wi[:, 448:]], axis=1).astype(BF16)
    n_mla = 640
    pm, pr = _inproj(h, row(norm_mix_g[0]), w_ext, n_mla, _tile(tp, 512))

    pos = jnp.maximum(jnp.arange(lp) - META0, 0).astype(F32)
    inv = ROPE_THETA ** (-jnp.arange(32, dtype=F32) / 32)
    ang = pos[:, None] * inv[None, :]
    zl = jnp.zeros((lp, 64), F32)
    cos = jnp.tile(jnp.concatenate([jnp.cos(ang), jnp.cos(ang), zl], axis=1), (batch, 1))
    sin = jnp.tile(jnp.concatenate([-jnp.sin(ang), jnp.sin(ang), zl], axis=1), (batch, 1))

    wuq = mla_w_uq[0].reshape(256, MLA_HEADS, 192)
    zq = jnp.zeros((256, MLA_HEADS, 64), F32)
    wqa = jnp.concatenate([wuq, zq], axis=2).reshape(256, MLA_HEADS * 256).astype(BF16)
    wqb = jnp.concatenate([wuq[:, :, 160:192], wuq[:, :, 128:160], zq], axis=2).reshape(256, MLA_HEADS * 128).astype(BF16)
    wukv = mla_w_ukv[0].reshape(128, MLA_HEADS, 256)
    wk = wukv[:, :, :128].reshape(128, MLA_HEADS * 128).astype(BF16)
    wv = wukv[:, :, 128:].reshape(128, MLA_HEADS * 128).astype(BF16)
    q, k, v = _mla_prep(pm, cos, sin, row(mla_q_norm_g[0]), row(mla_kv_norm_g[0]), wqa, wqb, wk, wv,
                        192 ** -0.5, _tile(tp, 512))
    o_mla = _attention(q, k, v, batch, lp, s_len)

    z_up = jnp.zeros((64, 512), F32)
    wup = jnp.concatenate([rwkv_w_up[0], z_up], axis=0).astype(BF16)
    aup = jnp.concatenate([z_up, rwkv_a_up[0]], axis=0).astype(BF16)
    head_id = jnp.arange(512) // RWKV_HEAD
    bd = (head_id[:, None] == head_id[None, :]).astype(BF16)
    r, lw, k2, vv, kn, bv, bonus, g = _rwkv_prep(
        pr, row(rwkv_mu[0]), row(rwkv_w0[0]), row(rwkv_a0[0]), row(rwkv_k_k[0]), row(rwkv_k_a[0]),
        row(rwkv_r_k[0]), wup, aup, rwkv_g_up[0].astype(BF16), bd, batch, lp, 256)
    o_rwkv = _rwkv_scan(r, lw, k2, vv, kn, bv, batch, lp, s_len)

    heads = peer_sub_keys.shape[1]
    h2, t, pq = _outproj(x.reshape(batch * s_len, d), o_mla, o_rwkv, bonus, g, row(mla_out_g[0]),
                         row(rwkv_ln_w[0]), row(rwkv_ln_b[0]), bd, w_out[0].astype(BF16), row(norm_ffn_g[0]),
                         peer_w_q[0].astype(BF16), batch, lp, s_len, _tile(s_len, 512))

    t_all = batch * s_len
    keys = peer_sub_keys[0].reshape(heads * 2, 128, 128).astype(BF16)
    tm = _tile(t_all, 512)
    s1, s2, e1, e2, tau = _route(pq, keys, heads, tm)
    y = _peer(t, peer_u[0].astype(BF16), peer_v[0].T.astype(BF16), s1, s2, e1, e2, tau, h2,
              row(norm_final_g), heads, tm, 1024)
    return y.reshape(batch, s_len, d)
```

```python
import functools
import math

import jax
import jax.numpy as jnp
from jax import lax
from jax.experimental import pallas as pl
from jax.experimental.pallas import tpu as pltpu

F32 = jnp.float32
BF16 = jnp.bfloat16
HIGHEST = lax.Precision.HIGHEST

RMS_EPS = 1e-6
GN_EPS = 64e-5
ROPE_THETA = 10000.0
N_META = 16
CHUNK = 64
X0 = 512
META0 = X0 - N_META
K0 = X0 - 128
SCAN_C0 = (X0 - CHUNK) // CHUNK
MLA_HEADS = 4
RWKV_HEAD = 64
PEER_TOPK = 16
VMEM_LIMIT = 56 * 1024 * 1024


def _dot(a, b, precision=None):
    return jnp.dot(a, b, preferred_element_type=F32, precision=precision)


def _dot_nt(a, b, precision=None):
    return lax.dot_general(a, b, (((1,), (1,)), ((), ())),
                           preferred_element_type=F32, precision=precision)


def _dot_split(x, m):
    hi = x.astype(BF16)
    lo = (x - hi.astype(F32)).astype(BF16)
    return _dot(hi, m) + _dot(lo, m)


def _rms(x, g):
    return x * lax.rsqrt(jnp.mean(x * x, axis=-1, keepdims=True) + RMS_EPS) * g


def _params(*sem):
    return pltpu.CompilerParams(dimension_semantics=sem, vmem_limit_bytes=VMEM_LIMIT)


def _full(shape):
    n = len(shape)
    return pl.BlockSpec(shape, lambda *_: (0,) * n)


def _inproj_kernel(h_ref, g_ref, w_ref, pm_ref, pr_ref, *, n_mla):
    n = _rms(h_ref[...], g_ref[...]).astype(BF16)
    p = _dot(n, w_ref[...])
    pm_ref[...] = p[:, :n_mla]
    pr_ref[...] = p[:, n_mla:]


def _inproj(h, g, w, n_mla, tm):
    tp, d = h.shape
    n_all = w.shape[1]
    return pl.pallas_call(
        functools.partial(_inproj_kernel, n_mla=n_mla),
        grid=(tp // tm,),
        in_specs=[pl.BlockSpec((tm, d), lambda i: (i, 0)), _full((1, d)), _full(w.shape)],
        out_specs=[pl.BlockSpec((tm, n_mla), lambda i: (i, 0)),
                   pl.BlockSpec((tm, n_all - n_mla), lambda i: (i, 0))],
        out_shape=[jax.ShapeDtypeStruct((tp, n_mla), F32),
                   jax.ShapeDtypeStruct((tp, n_all - n_mla), F32)],
        compiler_params=_params("parallel"),
        name="inproj",
    )(h, g, w)


def _mla_prep_kernel(pm_ref, cos_ref, sin_ref, gq_ref, gkv_ref, wqa_ref, wqb_ref, wk_ref, wv_ref,
                     q_ref, k_ref, v_ref, *, scale):
    pm = pm_ref[...]
    nq = _rms(pm[:, 0:256], gq_ref[...]).astype(BF16)
    nkv = _rms(pm[:, 256:384], gkv_ref[...]).astype(BF16)
    cos, sin = cos_ref[...], sin_ref[...]
    qa = _dot(nq, wqa_ref[...])
    qb = _dot(nq, wqb_ref[...])
    kn = _dot(nkv, wk_ref[...])
    vv = _dot(nkv, wv_ref[...])
    kr = pm[:, 384:512] * cos + pm[:, 512:640] * sin
    for h in range(MLA_HEADS):
        qn = qa[:, 256 * h:256 * h + 128]
        qr = qa[:, 256 * h + 128:256 * h + 256] * cos + qb[:, 128 * h:128 * h + 128] * sin
        q_ref[h] = (jnp.concatenate([qn, qr], axis=-1) * scale).astype(BF16)
        k_ref[h] = jnp.concatenate([kn[:, 128 * h:128 * h + 128], kr], axis=-1).astype(BF16)
        v_ref[h] = vv[:, 128 * h:128 * h + 128].astype(BF16)


def _mla_prep(pm, cos, sin, gq, gkv, wqa, wqb, wk, wv, scale, tm):
    tp = pm.shape[0]
    row = lambda w: pl.BlockSpec((tm, w), lambda i: (i, 0))
    head = lambda w: pl.BlockSpec((MLA_HEADS, tm, w), lambda i: (0, i, 0))
    return pl.pallas_call(
        functools.partial(_mla_prep_kernel, scale=scale),
        grid=(tp // tm,),
        in_specs=[row(pm.shape[1]), row(128), row(128), _full(gq.shape), _full(gkv.shape),
                  _full(wqa.shape), _full(wqb.shape), _full(wk.shape), _full(wv.shape)],
        out_specs=[head(256), head(256), head(128)],
        out_shape=[jax.ShapeDtypeStruct((MLA_HEADS, tp, 256), BF16),
                   jax.ShapeDtypeStruct((MLA_HEADS, tp, 256), BF16),
                   jax.ShapeDtypeStruct((MLA_HEADS, tp, 128), BF16)],
        compiler_params=_params("parallel"),
        name="mla_prep",
    )(pm, cos, sin, gq, gkv, wqa, wqb, wk, wv)


ATT_Q = 256


ATT_HEADS = 2


def _attn_kernel(q_ref, k_ref, v_ref, o_ref, *, n_q):
    heads = range(q_ref.shape[0])
    for qi in range(n_q):
        q0 = X0 + ATT_Q * qi
        kend = q0 + ATT_Q
        n = kend - K0
        kk = lax.broadcasted_iota(jnp.int32, (1, n), 1)
        ck = jnp.where(kk < 128 - N_META, 1 << 20, kk >> 6)
        cq = ((ATT_Q * qi + lax.broadcasted_iota(jnp.int32, (ATT_Q, 1), 0)) >> 6) + 2
        visible = ck <= cq
        s = [_dot_nt(q_ref[h, q0:q0 + ATT_Q, :], k_ref[h, K0:kend, :]) for h in heads]
        s = [jnp.where(visible, x, -1e30) for x in s]
        e = [jnp.exp(x - jnp.max(x, axis=-1, keepdims=True)) for x in s]
        l = [jnp.sum(x, axis=-1, keepdims=True) for x in e]
        o = [_dot(x.astype(BF16), v_ref[h, K0:kend, :]) for h, x in zip(heads, e)]
        for h in heads:
            o_ref[ATT_Q * qi:ATT_Q * (qi + 1), 128 * h:128 * (h + 1)] = o[h] / l[h]


def _attention(q, k, v, batch, lp, s_len):
    return pl.pallas_call(
        functools.partial(_attn_kernel, n_q=s_len // ATT_Q),
        grid=(batch, MLA_HEADS // ATT_HEADS),
        in_specs=[pl.BlockSpec((ATT_HEADS, lp, 256), lambda b, h: (h, b, 0)),
                  pl.BlockSpec((ATT_HEADS, lp, 256), lambda b, h: (h, b, 0)),
                  pl.BlockSpec((ATT_HEADS, lp, 128), lambda b, h: (h, b, 0))],
        out_specs=pl.BlockSpec((s_len, 128 * ATT_HEADS), lambda b, h: (b, h)),
        out_shape=jax.ShapeDtypeStruct((batch * s_len, MLA_HEADS * 128), F32),
        compiler_params=_params("parallel", "parallel"),
        name="mla_attn",
    )(q, k, v)


def _softplus(x):
    return jnp.maximum(x, 0.0) + jnp.log(1.0 + jnp.exp(-jnp.abs(x)))


def _sigmoid(x):
    return 1.0 / (1.0 + jnp.exp(-x))


def _rwkv_prep_kernel(p_ref, prev_ref, mu_ref, w0_ref, a0_ref, kk_ref, ka_ref, rk_ref,
                      wup_ref, aup_ref, gup_ref, bd_ref,
                      r_ref, lw_ref, k_ref, v_ref, kn_ref, b_ref, bonus_ref, g_ref, *, tm):
    p = p_ref[...]
    rows = lax.broadcasted_iota(jnp.int32, (tm, 1), 0)
    shifted = jnp.where(rows == 0, prev_ref[7:8, :], pltpu.roll(p, 1, 0))
    pm = p + mu_ref[...] * (shifted - p)
    r, k, v = pm[:, 0:512], pm[:, 512:1024], pm[:, 1024:1536]
    dd, dg = pm[:, 1536:1664], pm[:, 1664:1792]
    w = -_softplus(-(w0_ref[...] + _dot(jnp.tanh(dd).astype(BF16), wup_ref[...]))) - 0.5
    log_decay = -jnp.exp(w)
    a = _sigmoid(a0_ref[...] + _dot(dd.astype(BF16), aup_ref[...]))
    g_ref[...] = _dot(_sigmoid(dg).astype(BF16), gup_ref[...])
    bd = bd_ref[...]
    kk = k * kk_ref[...]
    norm = jnp.sqrt(_dot_split(kk * kk, bd))
    kk = kk / jnp.maximum(norm, 1e-12)
    k2 = k * (1.0 + (a - 1.0) * ka_ref[...])
    real = (pl.program_id(1) * tm + rows) >= META0
    r_ref[...] = r
    lw_ref[...] = jnp.where(real, log_decay, 0.0)
    k_ref[...] = k2
    v_ref[...] = v
    kn_ref[...] = kk
    b_ref[...] = kk * a
    bonus_ref[...] = _dot_split(r * k2 * rk_ref[...], bd) * v


def _rwkv_prep(pr, mu, w0, a0, k_k, k_a, r_k, wup, aup, gup, bd, batch, lp, tm):
    tp, width = pr.shape
    nl = lp // tm
    vec = lambda a: _full(a.shape)
    out = pl.BlockSpec((tm, 512), lambda b, j: (b * nl + j, 0))
    return pl.pallas_call(
        functools.partial(_rwkv_prep_kernel, tm=tm),
        grid=(batch, nl),
        in_specs=[pl.BlockSpec((tm, width), lambda b, j: (b * nl + j, 0)),
                  pl.BlockSpec((8, width), lambda b, j: (jnp.maximum((b * nl + j) * (tm // 8) - 1, 0), 0)),
                  vec(mu), vec(w0), vec(a0), vec(k_k), vec(k_a), vec(r_k),
                  vec(wup), vec(aup), vec(gup), vec(bd)],
        out_specs=[out] * 8,
        out_shape=[jax.ShapeDtypeStruct((tp, 512), F32)] * 8,
        compiler_params=_params("parallel", "parallel"),
        name="rwkv_prep",
    )(pr, pr, mu, w0, a0, k_k, k_a, r_k, wup, aup, gup, bd)


def _bdot(a, b):
    return _dot(a.astype(BF16), b.astype(BF16))


def _rwkv_scan_kernel(r_ref, lw_ref, k_ref, v_ref, kn_ref, b_ref, o_ref, z_ref):
    c = CHUNK
    w = 4 * c

    @pl.when(pl.program_id(1) == 0)
    def _():
        z_ref[...] = jnp.zeros_like(z_ref)

    ri = lax.broadcasted_iota(jnp.int32, (w, w), 0)
    ci = lax.broadcasted_iota(jnp.int32, (w, w), 1)
    same_head = (ri >> 6) == (ci >> 6)
    strict = (ri & 63) > (ci & 63)
    incl = (ri & 63) >= (ci & 63)
    eye = ri == ci
    tri = (lax.broadcasted_iota(jnp.int32, (c, c), 0) >= lax.broadcasted_iota(jnp.int32, (c, c), 1)).astype(F32)

    def stack(x):
        return jnp.where(same_head, jnp.concatenate([x, x, x, x], axis=0), 0.0)

    n_seq = lw_ref.shape[0]
    cum_all = [_dot(tri, lw_ref[p], HIGHEST) for p in range(n_seq)]
    groups = range(z_ref.shape[0])
    st = [dict() for _ in groups]
    for q in groups:
        p = q // 2
        lanes = slice(w * (q % 2), w * (q % 2) + w)
        lw = lw_ref[p, :, lanes]
        cum = cum_all[p][:, lanes]
        cum_prev = cum - lw
        mid = cum[c // 2 - 1:c // 2, :]
        tot = cum[c - 1:c, :]
        a = -kn_ref[p, :, lanes]
        b = b_ref[p, :, lanes]
        k = k_ref[p, :, lanes]
        r = r_ref[p, :, lanes]
        dn = jnp.exp(mid - cum)
        de = jnp.exp(tot - cum)
        a_t = stack(a * jnp.exp(cum_prev - mid))
        b_t = stack(b * dn)
        k_t = stack(k * dn)
        r_t = stack(r * jnp.exp(cum - mid))
        st[q].update(seq=p, lanes=lanes, tot=tot, a_0=stack(a * jnp.exp(cum_prev)), r_0=stack(r * jnp.exp(cum)),
                     b_e=stack(b * de), k_e=stack(k * de), vs=stack(v_ref[p, :, lanes]),
                     lhs=jnp.concatenate([a_t, r_t], axis=0).astype(BF16),
                     rhs=jnp.concatenate([b_t, k_t], axis=0).astype(BF16))
    for q in groups:
        inter = _dot_nt(st[q]["lhs"], st[q]["rhs"])
        l_ab = jnp.where(strict, inter[:w, :w], 0.0)
        st[q].update(l_ak=jnp.where(strict, inter[:w, w:], 0.0), m_rb=jnp.where(incl, inter[w:, :w], 0.0),
                     m_rk=jnp.where(incl, inter[w:, w:], 0.0),
                     t_inv=jnp.where(eye, 1.0, 0.0) + l_ab, l_ab=l_ab)
    for q in groups:
        st[q]["power"] = _bdot(st[q]["l_ab"], st[q]["l_ab"])
    for _ in range(int(math.log2(c)) - 2):
        for q in groups:
            both = _bdot(st[q]["power"], jnp.concatenate([st[q]["power"], st[q]["t_inv"]], axis=1))
            st[q]["power"], st[q]["t_inv"] = both[:, :w], st[q]["t_inv"] + both[:, w:]
    for q in groups:
        st[q]["t_inv"] = st[q]["t_inv"] + _bdot(st[q]["power"], st[q]["t_inv"])
    for q in groups:
        st[q]["pq"] = _bdot(st[q]["t_inv"], jnp.concatenate([st[q]["a_0"], st[q]["l_ak"]], axis=1))
    for q in groups:
        st[q]["mix"] = _bdot(jnp.concatenate([st[q]["m_rb"], st[q]["b_e"].T], axis=0), st[q]["pq"])
    for q in groups:
        mix, tot = st[q]["mix"], st[q]["tot"]
        r_hat = st[q]["r_0"] + mix[:w, :w]
        m_o = mix[:w, w:] + st[q]["m_rk"]
        g = jnp.where(eye, jnp.exp(tot), 0.0) + mix[w:, :w]
        m_h = mix[w:, w:] + st[q]["k_e"].T
        lhs = jnp.concatenate([jnp.concatenate([r_hat, m_o], axis=1), jnp.concatenate([g, m_h], axis=1)], axis=0)
        new = _bdot(lhs, jnp.concatenate([z_ref[q], st[q]["vs"]], axis=0))
        o_rows = new[:w]
        o_ref[st[q]["seq"], :, st[q]["lanes"]] = o_rows[0:c] + o_rows[c:2 * c] + o_rows[2 * c:3 * c] + o_rows[3 * c:4 * c]
        z_ref[q] = new[w:]


SCAN_SEQS = 2


def _rwkv_scan(r, lw, k, v, kn, b, batch, lp, s_len):
    nc = s_len // CHUNK + 1
    assert batch % SCAN_SEQS == 0
    pairs = batch // SCAN_SEQS
    ins = [a.reshape(pairs, SCAN_SEQS, lp, 512) for a in (r, lw, k, v, kn, b)]
    blk = pl.BlockSpec((None, SCAN_SEQS, CHUNK, 512), lambda bi, ci: (bi, 0, SCAN_C0 + ci, 0))
    out = pl.pallas_call(
        _rwkv_scan_kernel,
        grid=(pairs, nc),
        in_specs=[blk] * 6,
        out_specs=pl.BlockSpec((None, SCAN_SEQS, CHUNK, 512), lambda bi, ci: (bi, 0, jnp.maximum(ci - 1, 0), 0)),
        out_shape=jax.ShapeDtypeStruct((pairs, SCAN_SEQS, s_len, 512), F32),
        scratch_shapes=[pltpu.VMEM((2 * SCAN_SEQS, 256, 256), F32)],
        compiler_params=_params("parallel", "arbitrary"),
        name="rwkv_scan",
    )(*ins)
    return out.reshape(batch * s_len, 512)


def _outproj_kernel(x_ref, om_ref, orw_ref, bonus_ref, g_ref, og_ref, lnw_ref, lnb_ref, bd_ref,
                    wout_ref, gffn_ref, wq_ref, h2_ref, t_ref, q_ref):
    y_mla = _rms(om_ref[...], og_ref[...])
    o = orw_ref[...]
    bd = bd_ref[...]
    mean = _dot_split(o, bd) * (1.0 / RWKV_HEAD)
    cen = o - mean
    var = _dot_split(cen * cen, bd) * (1.0 / RWKV_HEAD)
    y_rwkv = (cen * lax.rsqrt(var + GN_EPS) * lnw_ref[...] + lnb_ref[...] + bonus_ref[...]) * g_ref[...]
    y = jnp.concatenate([y_mla, y_rwkv], axis=-1).astype(BF16)
    h2 = x_ref[...] + _dot(y, wout_ref[...])
    h2_ref[...] = h2
    t = _rms(h2, gffn_ref[...]).astype(BF16)
    t_ref[...] = t
    q_ref[...] = _dot(t, wq_ref[...])


def _outproj(x, o_mla, o_rwkv, bonus, g, og, lnw, lnb, bd, wout, gffn, wq, batch, lp, s_len, tm):
    t_all, d = x.shape
    ns = s_len // tm
    npad = lp // tm
    xrow = lambda w: pl.BlockSpec((tm, w), lambda b, j: (b * ns + j, 0))
    prow = lambda w: pl.BlockSpec((tm, w), lambda b, j: (b * npad + X0 // tm + j, 0))
    vec = lambda a: _full(a.shape)
    return pl.pallas_call(
        _outproj_kernel,
        grid=(batch, ns),
        in_specs=[xrow(d), xrow(512), xrow(512), prow(512), prow(512), vec(og), vec(lnw), vec(lnb), vec(bd),
                  vec(wout), vec(gffn), vec(wq)],
        out_specs=[xrow(d), xrow(d), xrow(wq.shape[1])],
        out_shape=[jax.ShapeDtypeStruct((t_all, d), F32), jax.ShapeDtypeStruct((t_all, d), BF16),
                   jax.ShapeDtypeStruct((t_all, wq.shape[1]), F32)],
        compiler_params=_params("parallel", "parallel"),
        name="outproj",
    )(x, o_mla, o_rwkv, bonus, g, og, lnw, lnb, bd, wout, gffn, wq)


_CANDS = [(i, j) for i in range(PEER_TOPK) for j in range(PEER_TOPK) if (i + 1) * (j + 1) <= PEER_TOPK]


def _exchange(v, i, l):
    v[i], v[l] = jnp.maximum(v[i], v[l]), jnp.minimum(v[i], v[l])


def _bitonic_merge_desc(v):
    n = len(v)
    j = n // 2
    while j >= 1:
        for i in range(n):
            if i ^ j > i:
                _exchange(v, i, i ^ j)
        j //= 2
    return v


def _bitonic_sort_desc(v):
    n = len(v)
    k = 2
    while k <= n:
        j = k // 2
        while j >= 1:
            for i in range(n):
                l = i ^ j
                if l > i:
                    if (i & k) == 0 or k == n:
                        _exchange(v, i, l)
                    else:
                        _exchange(v, l, i)
            j //= 2
        k *= 2
    return v


def _top16_rows(s):
    v = _bitonic_sort_desc([s[8 * k:8 * k + 8, :] for k in range(PEER_TOPK)])
    for shift in (4, 2, 1):
        other = [pltpu.roll(x, shift, 0) for x in v]
        v = _bitonic_merge_desc([jnp.maximum(v[i], other[PEER_TOPK - 1 - i]) for i in range(PEER_TOPK)])
    return v


def _top_distinct(x, weight, n):
    out = []
    for _ in range(n):
        m = jnp.max(x, axis=0, keepdims=True)
        eq = x == m
        cnt = jnp.sum(jnp.where(eq, weight, 0.0), axis=0, keepdims=True)
        x = jnp.where(eq, -jnp.inf, x)
        out.append((m, cnt))
    return out


def _route_kernel(q_ref, keys_ref, th_ref, s2_ref, e1_ref, e2_ref, *, heads, tm):
    qb = q_ref[...].astype(BF16)
    pad = 56 - len(_CANDS)
    neg = jnp.full((pad, tm), -jnp.inf, F32)
    cw = jnp.concatenate([jnp.ones((len(_CANDS), tm), F32), jnp.zeros((pad, tm), F32)], axis=0)
    for h in range(heads):
        s = [_dot_nt(keys_ref[2 * h + c], qb[:, 128 * (2 * h + c):128 * (2 * h + c + 1)]) for c in range(2)]
        top = [[x[0:1, :] for x in _top16_rows(s[c])] for c in range(2)]
        cv = jnp.concatenate([top[0][i] + top[1][j] for i, j in _CANDS] + [neg], axis=0)
        tau = jnp.full((1, tm), -jnp.inf, F32)
        seen = jnp.zeros((1, tm), F32)
        for m, cnt in _top_distinct(cv, cw, PEER_TOPK):
            seen = seen + cnt
            tau = jnp.maximum(tau, jnp.where(seen >= PEER_TOPK, m, -jnp.inf))
        m1, m2 = top[0][0], top[1][0]
        z = jnp.sum(jnp.where(cv >= tau, cw * jnp.exp(cv - (m1 + m2)), 0.0), axis=0, keepdims=True)
        e1 = jnp.exp(s[0] - m1) / z
        e2 = jnp.exp(s[1] - m2)
        second = jnp.concatenate(top[1], axis=0)
        theta = [jnp.min(jnp.where(second + top[0][i] >= tau, second, jnp.inf), axis=0, keepdims=True)
                 for i in range(PEER_TOPK)]
        th = jnp.full(s[0].shape, jnp.inf, F32)
        for i in range(PEER_TOPK):
            th = jnp.minimum(th, jnp.where(s[0] >= top[0][i], theta[i], jnp.inf))
        for lb in range(tm // 128):
            lanes = slice(128 * lb, 128 * lb + 128)
            th_ref[lb, h] = th[:, lanes]
            s2_ref[lb, h] = s[1][:, lanes]
            e1_ref[lb, h] = e1[:, lanes]
            e2_ref[lb, h] = e2[:, lanes]


def _route(q, keys, heads, tm):
    t_all = q.shape[0]
    big = pl.BlockSpec((tm // 128, heads, 128, 128), lambda i: (i, 0, 0, 0))
    return pl.pallas_call(
        functools.partial(_route_kernel, heads=heads, tm=tm),
        grid=(t_all // tm,),
        in_specs=[pl.BlockSpec((tm, q.shape[1]), lambda i: (i, 0)), _full(keys.shape)],
        out_specs=[big] * 4,
        out_shape=[jax.ShapeDtypeStruct((t_all // 128, heads, 128, 128), F32)] * 4,
        compiler_params=_params("parallel"),
        name="peer_route",
    )(q, keys)


GATE_GROUP = 2


def _peer_kernel(t_ref, u_ref, vt_ref, th_ref, s2_ref, e1_ref, e2_ref, h2_ref, gf_ref,
                 y_ref, acc_ref, gate_ref, *, heads, te):
    j = pl.program_id(1)
    tm = t_ref.shape[0]

    @pl.when(j == 0)
    def _():
        acc_ref[...] = jnp.zeros_like(acc_ref)

    tile = (8, 128)
    n_a = te // 128
    for a8 in range(0, n_a, 8):
        a_rows = pl.ds(pl.multiple_of(j * n_a + a8, 8), 8)
        for ag in range(0, 8, GATE_GROUP):

            def lane_block(lb, carry, a_rows=a_rows, ag=ag, a8=a8):
                acc = [[jnp.zeros(tile, F32) for _ in range(16)] for _ in range(GATE_GROUP)]
                for h in range(heads):
                    th = th_ref[lb, h, a_rows, :]
                    e1 = e1_ref[lb, h, a_rows, :]
                    thb = [jnp.broadcast_to(th[ag + ai:ag + ai + 1, :], tile) for ai in range(GATE_GROUP)]
                    e1b = [jnp.broadcast_to(e1[ag + ai:ag + ai + 1, :], tile) for ai in range(GATE_GROUP)]
                    for r in range(16):
                        s2 = s2_ref[lb, h, 8 * r:8 * r + 8, :]
                        e2 = e2_ref[lb, h, 8 * r:8 * r + 8, :]
                        for ai in range(GATE_GROUP):
                            acc[ai][r] = acc[ai][r] + jnp.where(s2 >= thb[ai], e2, 0.0) * e1b[ai]
                for ai in range(GATE_GROUP):
                    for r in range(16):
                        row = (a8 + ag + ai) * 128 + 8 * r
                        gate_ref[lb, row:row + 8, :] = acc[ai][r]
                return carry

            lax.fori_loop(0, tm // 128, lane_block, 0)

    hp = _dot_nt(u_ref[...], t_ref[...])
    gate = jnp.concatenate([gate_ref[lb] for lb in range(tm // 128)], axis=1)
    act = 0.5 * hp * (1.0 + lax.erf(hp * (1.0 / math.sqrt(2.0)))) * gate
    acc_ref[...] += _dot(vt_ref[...], act.astype(BF16))

    @pl.when(j == pl.num_programs(1) - 1)
    def _():
        y_ref[...] = _rms(h2_ref[...] + acc_ref[...].T, gf_ref[...])


def _peer(t, u, vt, th, s2, e1, e2, h2, gf, heads, tm, te):
    t_all, d = t.shape
    n_exp = u.shape[0]
    tok = pl.BlockSpec((tm // 128, heads, 128, 128), lambda i, j: (i, 0, 0, 0))
    return pl.pallas_call(
        functools.partial(_peer_kernel, heads=heads, te=te),
        grid=(t_all // tm, n_exp // te),
        in_specs=[pl.BlockSpec((tm, d), lambda i, j: (i, 0)),
                  pl.BlockSpec((te, d), lambda i, j: (j, 0)),
                  pl.BlockSpec((d, te), lambda i, j: (0, j)),
                  tok, tok, tok, tok,
                  pl.BlockSpec((tm, d), lambda i, j: (i, 0)),
                  _full(gf.shape)],
        out_specs=pl.BlockSpec((tm, d), lambda i, j: (i, 0)),
        out_shape=jax.ShapeDtypeStruct((t_all, d), F32),
        scratch_shapes=[pltpu.VMEM((d, tm), F32), pltpu.VMEM((tm // 128, te, 128), F32)],
        compiler_params=_params("parallel", "arbitrary"),
        name="peer_experts",
    )(t, u, vt, th, s2, e1, e2, h2, gf)


def _tile(n, cap):
    t = cap
    while n % t:
        t //= 2
    return t


def kernel(x, meta_tokens, norm_mix_g, w_in, mla_q_norm_g, mla_w_uq, mla_kv_norm_g, mla_w_ukv, mla_out_g, rwkv_mu, rwkv_w0, rwkv_w_up, rwkv_a0, rwkv_a_up, rwkv_g_up, rwkv_k_k, rwkv_k_a, rwkv_r_k, rwkv_ln_w, rwkv_ln_b, w_out, norm_ffn_g, peer_w_q, peer_sub_keys, peer_u, peer_v, norm_final_g):
    batch, s_len, d = x.shape
    assert w_in.shape[0] == 1, "one layer"
    assert s_len % ATT_Q == 0 and d == 1024
    lp = X0 + s_len
    tp = batch * lp
    row = lambda a: a.reshape(1, -1).astype(F32)

    h = jnp.concatenate([jnp.zeros((batch, META0, d), F32),
                         jnp.broadcast_to(meta_tokens[None].astype(F32), (batch, N_META, d)), x], axis=1)
    h = h.reshape(tp, d)

    wi = w_in[0]
    kr = wi[:, 384:448]
    z64 = jnp.zeros((d, 64), F32)
    w_ext = jnp.concatenate([wi[:, 0:384], kr, z64, kr[:, 32:], kr[:, :32], z64, wi[:, 448:]], axis=1).astype(BF16)
    n_mla = 640
    pm, pr = _inproj(h, row(norm_mix_g[0]), w_ext, n_mla, _tile(tp, 512))

    pos = jnp.maximum(jnp.arange(lp) - META0, 0).astype(F32)
    inv = ROPE_THETA ** (-jnp.arange(32, dtype=F32) / 32)
    ang = pos[:, None] * inv[None, :]
    zl = jnp.zeros((lp, 64), F32)
    cos = jnp.tile(jnp.concatenate([jnp.cos(ang), jnp.cos(ang), zl], axis=1), (batch, 1))
    sin = jnp.tile(jnp.concatenate([-jnp.sin(ang), jnp.sin(ang), zl], axis=1), (batch, 1))

    wuq = mla_w_uq[0].reshape(256, MLA_HEADS, 192)
    zq = jnp.zeros((256, MLA_HEADS, 64), F32)
    wqa = jnp.concatenate([wuq, zq], axis=2).reshape(256, MLA_HEADS * 256).astype(BF16)
    wqb = jnp.concatenate([wuq[:, :, 160:192], wuq[:, :, 128:160], zq], axis=2).reshape(256, MLA_HEADS * 128).astype(BF16)
    wukv = mla_w_ukv[0].reshape(128, MLA_HEADS, 256)
    wk = wukv[:, :, :128].reshape(128, MLA_HEADS * 128).astype(BF16)
    wv = wukv[:, :, 128:].reshape(128, MLA_HEADS * 128).astype(BF16)
    q, k, v = _mla_prep(pm, cos, sin, row(mla_q_norm_g[0]), row(mla_kv_norm_g[0]), wqa, wqb, wk, wv,
                        192 ** -0.5, _tile(tp, 512))
    o_mla = _attention(q, k, v, batch, lp, s_len)

    z_up = jnp.zeros((64, 512), F32)
    wup = jnp.concatenate([rwkv_w_up[0], z_up], axis=0).astype(BF16)
    aup = jnp.concatenate([z_up, rwkv_a_up[0]], axis=0).astype(BF16)
    head_id = jnp.arange(512) // RWKV_HEAD
    bd = (head_id[:, None] == head_id[None, :]).astype(BF16)
    r, lw, k2, vv, kn, bv, bonus, g = _rwkv_prep(
        pr, row(rwkv_mu[0]), row(rwkv_w0[0]), row(rwkv_a0[0]), row(rwkv_k_k[0]), row(rwkv_k_a[0]),
        row(rwkv_r_k[0]), wup, aup, rwkv_g_up[0].astype(BF16), bd, batch, lp, 256)
    o_rwkv = _rwkv_scan(r, lw, k2, vv, kn, bv, batch, lp, s_len)

    heads = peer_sub_keys.shape[1]
    h2, t, pq = _outproj(x.reshape(batch * s_len, d), o_mla, o_rwkv, bonus, g, row(mla_out_g[0]),
                         row(rwkv_ln_w[0]), row(rwkv_ln_b[0]), bd, w_out[0].astype(BF16), row(norm_ffn_g[0]),
                         peer_w_q[0].astype(BF16), batch, lp, s_len, _tile(s_len, 512))

    t_all = batch * s_len
    keys = peer_sub_keys[0].reshape(heads * 2, 128, 128).astype(BF16)
    tm = _tile(t_all, 512)
    th, s2, e1, e2 = _route(pq, keys, heads, tm)
    y = _peer(t, peer_u[0].astype(BF16), peer_v[0].T.astype(BF16), th, s2, e1, e2, h2,
              row(norm_final_g), heads, tm, 1024)
    return y.reshape(batch, s_len, d)
```

```python
import functools
import math

import jax
import jax.numpy as jnp
from jax import lax
from jax.experimental import pallas as pl
from jax.experimental.pallas import tpu as pltpu

F32 = jnp.float32
BF16 = jnp.bfloat16
HIGHEST = lax.Precision.HIGHEST

RMS_EPS = 1e-6
GN_EPS = 64e-5
ROPE_THETA = 10000.0
N_META = 16
CHUNK = 64
X0 = 512
META0 = X0 - N_META
K0 = X0 - 128
SCAN_C0 = (X0 - CHUNK) // CHUNK
MLA_HEADS = 4
RWKV_HEAD = 64
PEER_TOPK = 16
VMEM_LIMIT = 56 * 1024 * 1024


def _dot(a, b, precision=None):
    return jnp.dot(a, b, preferred_element_type=F32, precision=precision)


def _dot_nt(a, b, precision=None):
    return lax.dot_general(a, b, (((1,), (1,)), ((), ())),
                           preferred_element_type=F32, precision=precision)


def _dot_split(x, m):
    hi = x.astype(BF16)
    lo = (x - hi.astype(F32)).astype(BF16)
    return _dot(hi, m) + _dot(lo, m)


def _rms(x, g):
    return x * lax.rsqrt(jnp.mean(x * x, axis=-1, keepdims=True) + RMS_EPS) * g


def _params(*sem):
    return pltpu.CompilerParams(dimension_semantics=sem, vmem_limit_bytes=VMEM_LIMIT)


def _full(shape):
    n = len(shape)
    return pl.BlockSpec(shape, lambda *_: (0,) * n)


def _inproj_kernel(head_ref, x_ref, g_ref, w_ref, pm_ref, pr_ref, *, n_mla, n_head):
    def project(rows):
        n = _rms(rows, g_ref[...]).astype(BF16)
        p = _dot(n, w_ref[...])
        pm_ref[...] = p[:, :n_mla]
        pr_ref[...] = p[:, n_mla:]

    @pl.when(pl.program_id(1) < n_head)
    def _():
        project(head_ref[...])

    @pl.when(pl.program_id(1) >= n_head)
    def _():
        project(x_ref[...])


def _inproj(head, x, g, w, n_mla, tm):
    batch, s_len, d = x.shape
    n_head = head.shape[0] // tm
    nl = n_head + s_len // tm
    n_all = w.shape[1]
    return pl.pallas_call(
        functools.partial(_inproj_kernel, n_mla=n_mla, n_head=n_head),
        grid=(batch, nl),
        in_specs=[pl.BlockSpec((tm, d), lambda b, j: (jnp.minimum(j, n_head - 1), 0)),
                  pl.BlockSpec((None, tm, d), lambda b, j: (b, jnp.maximum(j - n_head, 0), 0)),
                  _full((1, d)), _full(w.shape)],
        out_specs=[pl.BlockSpec((tm, n_mla), lambda b, j: (b * nl + j, 0)),
                   pl.BlockSpec((tm, n_all - n_mla), lambda b, j: (b * nl + j, 0))],
        out_shape=[jax.ShapeDtypeStruct((batch * nl * tm, n_mla), F32),
                   jax.ShapeDtypeStruct((batch * nl * tm, n_all - n_mla), F32)],
        compiler_params=_params("parallel", "parallel"),
        name="inproj",
    )(head, x, g, w)


def _mla_prep_kernel(pm_ref, cos_ref, sin_ref, gq_ref, gkv_ref, wqa_ref, wqb_ref, wk_ref, wv_ref,
                     q_ref, k_ref, v_ref, *, scale):
    pm = pm_ref[...]
    nq = _rms(pm[:, 0:256], gq_ref[...]).astype(BF16)
    nkv = _rms(pm[:, 256:384], gkv_ref[...]).astype(BF16)
    cos, sin = cos_ref[...], sin_ref[...]
    qa = _dot(nq, wqa_ref[...])
    qb = _dot(nq, wqb_ref[...])
    kn = _dot(nkv, wk_ref[...])
    vv = _dot(nkv, wv_ref[...])
    kr = pm[:, 384:512] * cos + pm[:, 512:640] * sin
    for h in range(MLA_HEADS):
        qn = qa[:, 256 * h:256 * h + 128]
        qr = qa[:, 256 * h + 128:256 * h + 256] * cos + qb[:, 128 * h:128 * h + 128] * sin
        q_ref[h] = (jnp.concatenate([qn, qr], axis=-1) * scale).astype(BF16)
        k_ref[h] = jnp.concatenate([kn[:, 128 * h:128 * h + 128], kr], axis=-1).astype(BF16)
        v_ref[h] = vv[:, 128 * h:128 * h + 128].astype(BF16)


def _mla_prep(pm, cos, sin, gq, gkv, wqa, wqb, wk, wv, scale, tm):
    tp = pm.shape[0]
    row = lambda w: pl.BlockSpec((tm, w), lambda i: (i, 0))
    head = lambda w: pl.BlockSpec((MLA_HEADS, tm, w), lambda i: (0, i, 0))
    return pl.pallas_call(
        functools.partial(_mla_prep_kernel, scale=scale),
        grid=(tp // tm,),
        in_specs=[row(pm.shape[1]), row(128), row(128), _full(gq.shape), _full(gkv.shape),
                  _full(wqa.shape), _full(wqb.shape), _full(wk.shape), _full(wv.shape)],
        out_specs=[head(256), head(256), head(128)],
        out_shape=[jax.ShapeDtypeStruct((MLA_HEADS, tp, 256), BF16),
                   jax.ShapeDtypeStruct((MLA_HEADS, tp, 256), BF16),
                   jax.ShapeDtypeStruct((MLA_HEADS, tp, 128), BF16)],
        compiler_params=_params("parallel"),
        name="mla_prep",
    )(pm, cos, sin, gq, gkv, wqa, wqb, wk, wv)


ATT_Q = 256


ATT_HEADS = 2


def _attn_kernel(q_ref, k_ref, v_ref, o_ref, *, n_q):
    heads = range(q_ref.shape[0])
    for qi in range(n_q):
        q0 = X0 + ATT_Q * qi
        kend = q0 + ATT_Q
        n = kend - K0
        kk = lax.broadcasted_iota(jnp.int32, (1, n), 1)
        ck = jnp.where(kk < 128 - N_META, 1 << 20, kk >> 6)
        cq = ((ATT_Q * qi + lax.broadcasted_iota(jnp.int32, (ATT_Q, 1), 0)) >> 6) + 2
        visible = ck <= cq
        s = [_dot_nt(q_ref[h, q0:q0 + ATT_Q, :], k_ref[h, K0:kend, :]) for h in heads]
        s = [jnp.where(visible, x, -1e30) for x in s]
        e = [jnp.exp(x - jnp.max(x, axis=-1, keepdims=True)) for x in s]
        l = [jnp.sum(x, axis=-1, keepdims=True) for x in e]
        o = [_dot(x.astype(BF16), v_ref[h, K0:kend, :]) for h, x in zip(heads, e)]
        for h in heads:
            o_ref[ATT_Q * qi:ATT_Q * (qi + 1), 128 * h:128 * (h + 1)] = o[h] / l[h]


def _attention(q, k, v, batch, lp, s_len):
    return pl.pallas_call(
        functools.partial(_attn_kernel, n_q=s_len // ATT_Q),
        grid=(batch, MLA_HEADS // ATT_HEADS),
        in_specs=[pl.BlockSpec((ATT_HEADS, lp, 256), lambda b, h: (h, b, 0)),
                  pl.BlockSpec((ATT_HEADS, lp, 256), lambda b, h: (h, b, 0)),
                  pl.BlockSpec((ATT_HEADS, lp, 128), lambda b, h: (h, b, 0))],
        out_specs=pl.BlockSpec((s_len, 128 * ATT_HEADS), lambda b, h: (b, h)),
        out_shape=jax.ShapeDtypeStruct((batch * s_len, MLA_HEADS * 128), F32),
        compiler_params=_params("parallel", "parallel"),
        name="mla_attn",
    )(q, k, v)


def _softplus(x):
    return jnp.maximum(x, 0.0) + jnp.log(1.0 + jnp.exp(-jnp.abs(x)))


def _sigmoid(x):
    return 1.0 / (1.0 + jnp.exp(-x))


def _rwkv_prep_kernel(p_ref, prev_ref, mu_ref, w0_ref, a0_ref, kk_ref, ka_ref, rk_ref,
                      wup_ref, aup_ref, gup_ref, bd_ref,
                      r_ref, lw_ref, k_ref, v_ref, kn_ref, b_ref, bonus_ref, g_ref, *, tm):
    p = p_ref[...]
    rows = lax.broadcasted_iota(jnp.int32, (tm, 1), 0)
    shifted = jnp.where(rows == 0, prev_ref[7:8, :], pltpu.roll(p, 1, 0))
    pm = p + mu_ref[...] * (shifted - p)
    r, k, v = pm[:, 0:512], pm[:, 512:1024], pm[:, 1024:1536]
    dd, dg = pm[:, 1536:1664], pm[:, 1664:1792]
    w = -_softplus(-(w0_ref[...] + _dot(jnp.tanh(dd).astype(BF16), wup_ref[...]))) - 0.5
    log_decay = -jnp.exp(w)
    a = _sigmoid(a0_ref[...] + _dot(dd.astype(BF16), aup_ref[...]))
    g_ref[...] = _dot(_sigmoid(dg).astype(BF16), gup_ref[...])
    bd = bd_ref[...]
    kk = k * kk_ref[...]
    norm = jnp.sqrt(_dot_split(kk * kk, bd))
    kk = kk / jnp.maximum(norm, 1e-12)
    k2 = k * (1.0 + (a - 1.0) * ka_ref[...])
    real = (pl.program_id(1) * tm + rows) >= META0
    r_ref[...] = r
    lw_ref[...] = jnp.where(real, log_decay, 0.0)
    k_ref[...] = k2
    v_ref[...] = v
    kn_ref[...] = kk
    b_ref[...] = kk * a
    bonus_ref[...] = _dot_split(r * k2 * rk_ref[...], bd) * v


def _rwkv_prep(pr, mu, w0, a0, k_k, k_a, r_k, wup, aup, gup, bd, batch, lp, tm):
    tp, width = pr.shape
    nl = lp // tm
    vec = lambda a: _full(a.shape)
    out = pl.BlockSpec((tm, 512), lambda b, j: (b * nl + j, 0))
    return pl.pallas_call(
        functools.partial(_rwkv_prep_kernel, tm=tm),
        grid=(batch, nl),
        in_specs=[pl.BlockSpec((tm, width), lambda b, j: (b * nl + j, 0)),
                  pl.BlockSpec((8, width), lambda b, j: (jnp.maximum((b * nl + j) * (tm // 8) - 1, 0), 0)),
                  vec(mu), vec(w0), vec(a0), vec(k_k), vec(k_a), vec(r_k),
                  vec(wup), vec(aup), vec(gup), vec(bd)],
        out_specs=[out] * 8,
        out_shape=[jax.ShapeDtypeStruct((tp, 512), F32)] * 8,
        compiler_params=_params("parallel", "parallel"),
        name="rwkv_prep",
    )(pr, pr, mu, w0, a0, k_k, k_a, r_k, wup, aup, gup, bd)


def _bdot(a, b):
    return _dot(a.astype(BF16), b.astype(BF16))


def _rwkv_scan_kernel(r_ref, lw_ref, k_ref, v_ref, kn_ref, b_ref, o_ref, z_ref):
    c = CHUNK
    w = 4 * c

    @pl.when(pl.program_id(1) == 0)
    def _():
        z_ref[...] = jnp.zeros_like(z_ref)

    ri = lax.broadcasted_iota(jnp.int32, (w, w), 0)
    ci = lax.broadcasted_iota(jnp.int32, (w, w), 1)
    same_head = (ri >> 6) == (ci >> 6)
    strict = (ri & 63) > (ci & 63)
    incl = (ri & 63) >= (ci & 63)
    eye = ri == ci
    tri = (lax.broadcasted_iota(jnp.int32, (c, c), 0) >= lax.broadcasted_iota(jnp.int32, (c, c), 1)).astype(F32)

    def stack(x):
        return jnp.where(same_head, jnp.concatenate([x, x, x, x], axis=0), 0.0)

    n_seq = lw_ref.shape[0]
    cum_all = [_dot(tri, lw_ref[p], HIGHEST) for p in range(n_seq)]
    groups = range(z_ref.shape[0])
    st = [dict() for _ in groups]
    for q in groups:
        p = q // 2
        lanes = slice(w * (q % 2), w * (q % 2) + w)
        lw = lw_ref[p, :, lanes]
        cum = cum_all[p][:, lanes]
        cum_prev = cum - lw
        mid = cum[c // 2 - 1:c // 2, :]
        tot = cum[c - 1:c, :]
        a = -kn_ref[p, :, lanes]
        b = b_ref[p, :, lanes]
        k = k_ref[p, :, lanes]
        r = r_ref[p, :, lanes]
        dn = jnp.exp(mid - cum)
        de = jnp.exp(tot - cum)
        a_t = stack(a * jnp.exp(cum_prev - mid))
        b_t = stack(b * dn)
        k_t = stack(k * dn)
        r_t = stack(r * jnp.exp(cum - mid))
        st[q].update(seq=p, lanes=lanes, tot=tot, a_0=stack(a * jnp.exp(cum_prev)), r_0=stack(r * jnp.exp(cum)),
                     b_e=stack(b * de), k_e=stack(k * de), vs=stack(v_ref[p, :, lanes]),
                     lhs=jnp.concatenate([a_t, r_t], axis=0).astype(BF16),
                     rhs=jnp.concatenate([b_t, k_t], axis=0).astype(BF16))
    for q in groups:
        inter = _dot_nt(st[q]["lhs"], st[q]["rhs"])
        l_ab = jnp.where(strict, inter[:w, :w], 0.0)
        st[q].update(l_ak=jnp.where(strict, inter[:w, w:], 0.0), m_rb=jnp.where(incl, inter[w:, :w], 0.0),
                     m_rk=jnp.where(incl, inter[w:, w:], 0.0),
                     t_inv=jnp.where(eye, 1.0, 0.0) + l_ab, l_ab=l_ab)
    for q in groups:
        st[q]["power"] = _bdot(st[q]["l_ab"], st[q]["l_ab"])
    for _ in range(int(math.log2(c)) - 2):
        for q in groups:
            both = _bdot(st[q]["power"], jnp.concatenate([st[q]["power"], st[q]["t_inv"]], axis=1))
            st[q]["power"], st[q]["t_inv"] = both[:, :w], st[q]["t_inv"] + both[:, w:]
    for q in groups:
        st[q]["t_inv"] = st[q]["t_inv"] + _bdot(st[q]["power"], st[q]["t_inv"])
    for q in groups:
        st[q]["pq"] = _bdot(st[q]["t_inv"], jnp.concatenate([st[q]["a_0"], st[q]["l_ak"]], axis=1))
    for q in groups:
        st[q]["mix"] = _bdot(jnp.concatenate([st[q]["m_rb"], st[q]["b_e"].T], axis=0), st[q]["pq"])
    for q in groups:
        mix, tot = st[q]["mix"], st[q]["tot"]
        r_hat = st[q]["r_0"] + mix[:w, :w]
        m_o = mix[:w, w:] + st[q]["m_rk"]
        g = jnp.where(eye, jnp.exp(tot), 0.0) + mix[w:, :w]
        m_h = mix[w:, w:] + st[q]["k_e"].T
        lhs = jnp.concatenate([jnp.concatenate([r_hat, m_o], axis=1), jnp.concatenate([g, m_h], axis=1)], axis=0)
        new = _bdot(lhs, jnp.concatenate([z_ref[q], st[q]["vs"]], axis=0))
        o_rows = new[:w]
        o_ref[st[q]["seq"], :, st[q]["lanes"]] = o_rows[0:c] + o_rows[c:2 * c] + o_rows[2 * c:3 * c] + o_rows[3 * c:4 * c]
        z_ref[q] = new[w:]


SCAN_SEQS = 2


def _rwkv_scan(r, lw, k, v, kn, b, batch, lp, s_len):
    nc = s_len // CHUNK + 1
    assert batch % SCAN_SEQS == 0
    pairs = batch // SCAN_SEQS
    ins = [a.reshape(pairs, SCAN_SEQS, lp, 512) for a in (r, lw, k, v, kn, b)]
    blk = pl.BlockSpec((None, SCAN_SEQS, CHUNK, 512), lambda bi, ci: (bi, 0, SCAN_C0 + ci, 0))
    out = pl.pallas_call(
        _rwkv_scan_kernel,
        grid=(pairs, nc),
        in_specs=[blk] * 6,
        out_specs=pl.BlockSpec((None, SCAN_SEQS, CHUNK, 512), lambda bi, ci: (bi, 0, jnp.maximum(ci - 1, 0), 0)),
        out_shape=jax.ShapeDtypeStruct((pairs, SCAN_SEQS, s_len, 512), F32),
        scratch_shapes=[pltpu.VMEM((2 * SCAN_SEQS, 256, 256), F32)],
        compiler_params=_params("parallel", "arbitrary"),
        name="rwkv_scan",
    )(*ins)
    return out.reshape(batch * s_len, 512)


def _outproj_kernel(x_ref, om_ref, orw_ref, bonus_ref, g_ref, og_ref, lnw_ref, lnb_ref, bd_ref,
                    wout_ref, gffn_ref, wq_ref, h2_ref, t_ref, q_ref):
    y_mla = _rms(om_ref[...], og_ref[...])
    o = orw_ref[...]
    bd = bd_ref[...]
    mean = _dot_split(o, bd) * (1.0 / RWKV_HEAD)
    cen = o - mean
    var = _dot_split(cen * cen, bd) * (1.0 / RWKV_HEAD)
    y_rwkv = (cen * lax.rsqrt(var + GN_EPS) * lnw_ref[...] + lnb_ref[...] + bonus_ref[...]) * g_ref[...]
    y = jnp.concatenate([y_mla, y_rwkv], axis=-1).astype(BF16)
    h2 = x_ref[...] + _dot(y, wout_ref[...])
    h2_ref[...] = h2
    t = _rms(h2, gffn_ref[...]).astype(BF16)
    t_ref[...] = t
    q_ref[...] = _dot(t, wq_ref[...])


def _outproj(x, o_mla, o_rwkv, bonus, g, og, lnw, lnb, bd, wout, gffn, wq, batch, lp, s_len, tm):
    t_all, d = x.shape
    ns = s_len // tm
    npad = lp // tm
    xrow = lambda w: pl.BlockSpec((tm, w), lambda b, j: (b * ns + j, 0))
    prow = lambda w: pl.BlockSpec((tm, w), lambda b, j: (b * npad + X0 // tm + j, 0))
    vec = lambda a: _full(a.shape)
    return pl.pallas_call(
        _outproj_kernel,
        grid=(batch, ns),
        in_specs=[xrow(d), xrow(512), xrow(512), prow(512), prow(512), vec(og), vec(lnw), vec(lnb), vec(bd),
                  vec(wout), vec(gffn), vec(wq)],
        out_specs=[xrow(d), xrow(d), xrow(wq.shape[1])],
        out_shape=[jax.ShapeDtypeStruct((t_all, d), F32), jax.ShapeDtypeStruct((t_all, d), BF16),
                   jax.ShapeDtypeStruct((t_all, wq.shape[1]), F32)],
        compiler_params=_params("parallel", "parallel"),
        name="outproj",
    )(x, o_mla, o_rwkv, bonus, g, og, lnw, lnb, bd, wout, gffn, wq)


_CANDS = [(i, j) for i in range(PEER_TOPK) for j in range(PEER_TOPK) if (i + 1) * (j + 1) <= PEER_TOPK]


def _exchange(v, i, l):
    v[i], v[l] = jnp.maximum(v[i], v[l]), jnp.minimum(v[i], v[l])


def _bitonic_merge_desc(v):
    n = len(v)
    j = n // 2
    while j >= 1:
        for i in range(n):
            if i ^ j > i:
                _exchange(v, i, i ^ j)
        j //= 2
    return v


def _bitonic_sort_desc(v):
    n = len(v)
    k = 2
    while k <= n:
        j = k // 2
        while j >= 1:
            for i in range(n):
                l = i ^ j
                if l > i:
                    if (i & k) == 0 or k == n:
                        _exchange(v, i, l)
                    else:
                        _exchange(v, l, i)
            j //= 2
        k *= 2
    return v


def _top16_rows(s):
    v = _bitonic_sort_desc([s[8 * k:8 * k + 8, :] for k in range(PEER_TOPK)])
    for shift in (4, 2, 1):
        other = [pltpu.roll(x, shift, 0) for x in v]
        v = _bitonic_merge_desc([jnp.maximum(v[i], other[PEER_TOPK - 1 - i]) for i in range(PEER_TOPK)])
    return v


def _top_distinct(x, weight, n):
    out = []
    for _ in range(n):
        m = jnp.max(x, axis=0, keepdims=True)
        eq = x == m
        cnt = jnp.sum(jnp.where(eq, weight, 0.0), axis=0, keepdims=True)
        x = jnp.where(eq, -jnp.inf, x)
        out.append((m, cnt))
    return out


def _route_kernel(q_ref, keys_ref, th_ref, s2_ref, e1_ref, e2_ref, *, heads, tm):
    qb = q_ref[...].astype(BF16)
    pad = 56 - len(_CANDS)
    neg = jnp.full((pad, tm), -jnp.inf, F32)
    cw = jnp.concatenate([jnp.ones((len(_CANDS), tm), F32), jnp.zeros((pad, tm), F32)], axis=0)
    for h in range(heads):
        s = [_dot_nt(keys_ref[2 * h + c], qb[:, 128 * (2 * h + c):128 * (2 * h + c + 1)]) for c in range(2)]
        top = [[x[0:1, :] for x in _top16_rows(s[c])] for c in range(2)]
        cv = jnp.concatenate([top[0][i] + top[1][j] for i, j in _CANDS] + [neg], axis=0)
        tau = jnp.full((1, tm), -jnp.inf, F32)
        seen = jnp.zeros((1, tm), F32)
        for m, cnt in _top_distinct(cv, cw, PEER_TOPK):
            seen = seen + cnt
            tau = jnp.maximum(tau, jnp.where(seen >= PEER_TOPK, m, -jnp.inf))
        m1, m2 = top[0][0], top[1][0]
        z = jnp.sum(jnp.where(cv >= tau, cw * jnp.exp(cv - (m1 + m2)), 0.0), axis=0, keepdims=True)
        e1 = jnp.exp(s[0] - m1) / z
        e2 = jnp.exp(s[1] - m2)
        second = jnp.concatenate(top[1], axis=0)
        theta = [jnp.min(jnp.where(second + top[0][i] >= tau, second, jnp.inf), axis=0, keepdims=True)
                 for i in range(PEER_TOPK)]
        th = jnp.full(s[0].shape, jnp.inf, F32)
        for i in range(PEER_TOPK):
            th = jnp.minimum(th, jnp.where(s[0] >= top[0][i], theta[i], jnp.inf))
        for lb in range(tm // 128):
            lanes = slice(128 * lb, 128 * lb + 128)
            th_ref[lb, h] = th[:, lanes]
            s2_ref[lb, h] = s[1][:, lanes]
            e1_ref[lb, h] = e1[:, lanes]
            e2_ref[lb, h] = e2[:, lanes]


def _route(q, keys, heads, tm):
    t_all = q.shape[0]
    big = pl.BlockSpec((tm // 128, heads, 128, 128), lambda i: (i, 0, 0, 0))
    return pl.pallas_call(
        functools.partial(_route_kernel, heads=heads, tm=tm),
        grid=(t_all // tm,),
        in_specs=[pl.BlockSpec((tm, q.shape[1]), lambda i: (i, 0)), _full(keys.shape)],
        out_specs=[big] * 4,
        out_shape=[jax.ShapeDtypeStruct((t_all // 128, heads, 128, 128), F32)] * 4,
        compiler_params=_params("parallel"),
        name="peer_route",
    )(q, keys)


GATE_GROUP = 2


def _peer_kernel(t_ref, u_ref, vt_ref, th_ref, s2_ref, e1_ref, e2_ref, h2_ref, gf_ref,
                 y_ref, acc_ref, gate_ref, *, heads, te):
    j = pl.program_id(1)
    tm = t_ref.shape[0]

    @pl.when(j == 0)
    def _():
        acc_ref[...] = jnp.zeros_like(acc_ref)

    tile = (8, 128)
    n_a = te // 128
    for a8 in range(0, n_a, 8):
        a_rows = pl.ds(pl.multiple_of(j * n_a + a8, 8), 8)
        for ag in range(0, 8, GATE_GROUP):

            def lane_block(lb, carry, a_rows=a_rows, ag=ag, a8=a8):
                acc = [[jnp.zeros(tile, F32) for _ in range(16)] for _ in range(GATE_GROUP)]
                for h in range(heads):
                    th = th_ref[lb, h, a_rows, :]
                    e1 = e1_ref[lb, h, a_rows, :]
                    thb = [jnp.broadcast_to(th[ag + ai:ag + ai + 1, :], tile) for ai in range(GATE_GROUP)]
                    e1b = [jnp.broadcast_to(e1[ag + ai:ag + ai + 1, :], tile) for ai in range(GATE_GROUP)]
                    for r in range(16):
                        s2 = s2_ref[lb, h, 8 * r:8 * r + 8, :]
                        e2 = e2_ref[lb, h, 8 * r:8 * r + 8, :]
                        for ai in range(GATE_GROUP):
                            acc[ai][r] = acc[ai][r] + jnp.where(s2 >= thb[ai], e2, 0.0) * e1b[ai]
                for ai in range(GATE_GROUP):
                    for r in range(16):
                        row = (a8 + ag + ai) * 128 + 8 * r
                        gate_ref[lb, row:row + 8, :] = acc[ai][r]
                return carry

            lax.fori_loop(0, tm // 128, lane_block, 0)

    hp = _dot_nt(u_ref[...], t_ref[...])
    gate = jnp.concatenate([gate_ref[lb] for lb in range(tm // 128)], axis=1)
    act = 0.5 * hp * (1.0 + lax.erf(hp * (1.0 / math.sqrt(2.0)))) * gate
    acc_ref[...] += _dot(vt_ref[...], act.astype(BF16))

    @pl.when(j == pl.num_programs(1) - 1)
    def _():
        y_ref[...] = _rms(h2_ref[...] + acc_ref[...].T, gf_ref[...])


def _peer(t, u, vt, th, s2, e1, e2, h2, gf, heads, tm, te):
    t_all, d = t.shape
    n_exp = u.shape[0]
    tok = pl.BlockSpec((tm // 128, heads, 128, 128), lambda i, j: (i, 0, 0, 0))
    return pl.pallas_call(
        functools.partial(_peer_kernel, heads=heads, te=te),
        grid=(t_all // tm, n_exp // te),
        in_specs=[pl.BlockSpec((tm, d), lambda i, j: (i, 0)),
                  pl.BlockSpec((te, d), lambda i, j: (j, 0)),
                  pl.BlockSpec((d, te), lambda i, j: (0, j)),
                  tok, tok, tok, tok,
                  pl.BlockSpec((tm, d), lambda i, j: (i, 0)),
                  _full(gf.shape)],
        out_specs=pl.BlockSpec((tm, d), lambda i, j: (i, 0)),
        out_shape=jax.ShapeDtypeStruct((t_all, d), F32),
        scratch_shapes=[pltpu.VMEM((d, tm), F32), pltpu.VMEM((tm // 128, te, 128), F32)],
        compiler_params=_params("parallel", "arbitrary"),
        name="peer_experts",
    )(t, u, vt, th, s2, e1, e2, h2, gf)


def _tile(n, cap):
    t = cap
    while n % t:
        t //= 2
    return t


def kernel(x, meta_tokens, norm_mix_g, w_in, mla_q_norm_g, mla_w_uq, mla_kv_norm_g, mla_w_ukv, mla_out_g, rwkv_mu, rwkv_w0, rwkv_w_up, rwkv_a0, rwkv_a_up, rwkv_g_up, rwkv_k_k, rwkv_k_a, rwkv_r_k, rwkv_ln_w, rwkv_ln_b, w_out, norm_ffn_g, peer_w_q, peer_sub_keys, peer_u, peer_v, norm_final_g):
    batch, s_len, d = x.shape
    assert w_in.shape[0] == 1, "one layer"
    assert s_len % ATT_Q == 0 and d == 1024
    lp = X0 + s_len
    tp = batch * lp
    row = lambda a: a.reshape(1, -1).astype(F32)

    head = jnp.concatenate([jnp.zeros((META0, d), F32), meta_tokens.astype(F32)], axis=0)

    wi = w_in[0]
    kr = wi[:, 384:448]
    z64 = jnp.zeros((d, 64), F32)
    w_ext = jnp.concatenate([wi[:, 0:384], kr, z64, kr[:, 32:], kr[:, :32], z64, wi[:, 448:]], axis=1).astype(BF16)
    n_mla = 640
    pm, pr = _inproj(head, x, row(norm_mix_g[0]), w_ext, n_mla, _tile(s_len, 512))

    pos = jnp.maximum(jnp.arange(lp) - META0, 0).astype(F32)
    inv = ROPE_THETA ** (-jnp.arange(32, dtype=F32) / 32)
    ang = pos[:, None] * inv[None, :]
    zl = jnp.zeros((lp, 64), F32)
    cos = jnp.tile(jnp.concatenate([jnp.cos(ang), jnp.cos(ang), zl], axis=1), (batch, 1))
    sin = jnp.tile(jnp.concatenate([-jnp.sin(ang), jnp.sin(ang), zl], axis=1), (batch, 1))

    wuq = mla_w_uq[0].reshape(256, MLA_HEADS, 192)
    zq = jnp.zeros((256, MLA_HEADS, 64), F32)
    wqa = jnp.concatenate([wuq, zq], axis=2).reshape(256, MLA_HEADS * 256).astype(BF16)
    wqb = jnp.concatenate([wuq[:, :, 160:192], wuq[:, :, 128:160], zq], axis=2).reshape(256, MLA_HEADS * 128).astype(BF16)
    wukv = mla_w_ukv[0].reshape(128, MLA_HEADS, 256)
    wk = wukv[:, :, :128].reshape(128, MLA_HEADS * 128).astype(BF16)
    wv = wukv[:, :, 128:].reshape(128, MLA_HEADS * 128).astype(BF16)
    q, k, v = _mla_prep(pm, cos, sin, row(mla_q_norm_g[0]), row(mla_kv_norm_g[0]), wqa, wqb, wk, wv,
                        192 ** -0.5, _tile(tp, 512))
    o_mla = _attention(q, k, v, batch, lp, s_len)

    z_up = jnp.zeros((64, 512), F32)
    wup = jnp.concatenate([rwkv_w_up[0], z_up], axis=0).astype(BF16)
    aup = jnp.concatenate([z_up, rwkv_a_up[0]], axis=0).astype(BF16)
    head_id = jnp.arange(512) // RWKV_HEAD
    bd = (head_id[:, None] == head_id[None, :]).astype(BF16)
    r, lw, k2, vv, kn, bv, bonus, g = _rwkv_prep(
        pr, row(rwkv_mu[0]), row(rwkv_w0[0]), row(rwkv_a0[0]), row(rwkv_k_k[0]), row(rwkv_k_a[0]),
        row(rwkv_r_k[0]), wup, aup, rwkv_g_up[0].astype(BF16), bd, batch, lp, 256)
    o_rwkv = _rwkv_scan(r, lw, k2, vv, kn, bv, batch, lp, s_len)

    heads = peer_sub_keys.shape[1]
    h2, t, pq = _outproj(x.reshape(batch * s_len, d), o_mla, o_rwkv, bonus, g, row(mla_out_g[0]),
                         row(rwkv_ln_w[0]), row(rwkv_ln_b[0]), bd, w_out[0].astype(BF16), row(norm_ffn_g[0]),
                         peer_w_q[0].astype(BF16), batch, lp, s_len, _tile(s_len, 512))

    t_all = batch * s_len
    keys = peer_sub_keys[0].reshape(heads * 2, 128, 128).astype(BF16)
    tm = _tile(t_all, 512)
    th, s2, e1, e2 = _route(pq, keys, heads, tm)
    y = _peer(t, peer_u[0].astype(BF16), peer_v[0].T.astype(BF16), th, s2, e1, e2, h2,
              row(norm_final_g), heads, tm, 1024)
    return y.reshape(batch, s_len, d)
```

```python
import functools
import math

import jax
import jax.numpy as jnp
from jax import lax
from jax.experimental import pallas as pl
from jax.experimental.pallas import tpu as pltpu

F32 = jnp.float32
BF16 = jnp.bfloat16
HIGHEST = lax.Precision.HIGHEST

RMS_EPS = 1e-6
GN_EPS = 64e-5
ROPE_THETA = 10000.0
N_META = 16
CHUNK = 64
X0 = 512
META0 = X0 - N_META
K0 = X0 - 128
SCAN_C0 = (X0 - CHUNK) // CHUNK
MLA_HEADS = 4
RWKV_HEAD = 64
PEER_TOPK = 16
VMEM_LIMIT = 56 * 1024 * 1024


def _dot(a, b, precision=None):
    return jnp.dot(a, b, preferred_element_type=F32, precision=precision)


def _dot_nt(a, b, precision=None):
    return lax.dot_general(a, b, (((1,), (1,)), ((), ())),
                           preferred_element_type=F32, precision=precision)


def _dot_split(x, m):
    hi = x.astype(BF16)
    lo = (x - hi.astype(F32)).astype(BF16)
    return _dot(hi, m) + _dot(lo, m)


def _rms(x, g):
    return x * lax.rsqrt(jnp.mean(x * x, axis=-1, keepdims=True) + RMS_EPS) * g


def _params(*sem):
    return pltpu.CompilerParams(dimension_semantics=sem, vmem_limit_bytes=VMEM_LIMIT)


def _full(shape):
    n = len(shape)
    return pl.BlockSpec(shape, lambda *_: (0,) * n)


def _softplus(x):
    return jnp.maximum(x, 0.0) + jnp.log(1.0 + jnp.exp(-jnp.abs(x)))


def _sigmoid(x):
    return 1.0 / (1.0 + jnp.exp(-x))


def _rwkv_prep(p, prev_ref, row0, params, outs):
    mu_ref, w0_ref, a0_ref, kk_ref, ka_ref, rk_ref, wup_ref, aup_ref, gup_ref, bd_ref = params
    r_ref, lw_ref, k_ref, v_ref, kn_ref, b_ref, bonus_ref, g_ref = outs
    tm = p.shape[0]
    rows = lax.broadcasted_iota(jnp.int32, (tm, 1), 0)
    shifted = jnp.where(rows == 0, prev_ref[7:8, :], pltpu.roll(p, 1, 0))
    prev_ref[...] = p[tm - 8:tm, :]
    pm = p + mu_ref[...] * (shifted - p)
    r, k, v = pm[:, 0:512], pm[:, 512:1024], pm[:, 1024:1536]
    dd, dg = pm[:, 1536:1664], pm[:, 1664:1792]
    w = -_softplus(-(w0_ref[...] + _dot(jnp.tanh(dd).astype(BF16), wup_ref[...]))) - 0.5
    log_decay = -jnp.exp(w)
    a = _sigmoid(a0_ref[...] + _dot(dd.astype(BF16), aup_ref[...]))
    g_ref[...] = _dot(_sigmoid(dg).astype(BF16), gup_ref[...])
    bd = bd_ref[...]
    kk = k * kk_ref[...]
    norm = jnp.sqrt(_dot_split(kk * kk, bd))
    kk = kk / jnp.maximum(norm, 1e-12)
    k2 = k * (1.0 + (a - 1.0) * ka_ref[...])
    real = (row0 + rows) >= META0
    r_ref[...] = r
    lw_ref[...] = jnp.where(real, log_decay, 0.0)
    k_ref[...] = k2
    v_ref[...] = v
    kn_ref[...] = kk
    b_ref[...] = kk * a
    bonus_ref[...] = _dot_split(r * k2 * rk_ref[...], bd) * v


def _inproj_kernel(head_ref, x_ref, g_ref, w_ref, mu_ref, w0_ref, a0_ref, kk_ref, ka_ref, rk_ref,
                   wup_ref, aup_ref, gup_ref, bd_ref,
                   pm_ref, r_ref, lw_ref, k_ref, v_ref, kn_ref, b_ref, bonus_ref, gate_ref, prev_ref,
                   *, n_mla, n_head, tm):
    j = pl.program_id(1)

    @pl.when(j == 0)
    def _():
        prev_ref[...] = jnp.zeros_like(prev_ref)

    def project(rows):
        n = _rms(rows, g_ref[...]).astype(BF16)
        p = _dot(n, w_ref[...])
        pm_ref[...] = p[:, :n_mla]
        _rwkv_prep(p[:, n_mla:], prev_ref, j * tm,
                   (mu_ref, w0_ref, a0_ref, kk_ref, ka_ref, rk_ref, wup_ref, aup_ref, gup_ref, bd_ref),
                   (r_ref, lw_ref, k_ref, v_ref, kn_ref, b_ref, bonus_ref, gate_ref))

    @pl.when(j < n_head)
    def _():
        project(head_ref[...])

    @pl.when(j >= n_head)
    def _():
        project(x_ref[...])


def _inproj(head, x, g, w, rwkv, n_mla, tm):
    batch, s_len, d = x.shape
    n_head = head.shape[0] // tm
    nl = n_head + s_len // tm
    rows = batch * nl * tm
    out = lambda width: pl.BlockSpec((tm, width), lambda b, j: (b * nl + j, 0))
    return pl.pallas_call(
        functools.partial(_inproj_kernel, n_mla=n_mla, n_head=n_head, tm=tm),
        grid=(batch, nl),
        in_specs=[pl.BlockSpec((tm, d), lambda b, j: (jnp.minimum(j, n_head - 1), 0)),
                  pl.BlockSpec((None, tm, d), lambda b, j: (b, jnp.maximum(j - n_head, 0), 0)),
                  _full((1, d)), _full(w.shape)] + [_full(a.shape) for a in rwkv],
        out_specs=[out(n_mla)] + [out(512)] * 8,
        out_shape=[jax.ShapeDtypeStruct((rows, n_mla), F32)] + [jax.ShapeDtypeStruct((rows, 512), F32)] * 8,
        scratch_shapes=[pltpu.VMEM((8, w.shape[1] - n_mla), F32)],
        compiler_params=_params("parallel", "arbitrary"),
        name="inproj",
    )(head, x, g, w, *rwkv)


def _mla_prep_kernel(pm_ref, cos_ref, sin_ref, gq_ref, gkv_ref, wqa_ref, wqb_ref, wk_ref, wv_ref,
                     q_ref, k_ref, v_ref, *, scale):
    pm = pm_ref[...]
    nq = _rms(pm[:, 0:256], gq_ref[...]).astype(BF16)
    nkv = _rms(pm[:, 256:384], gkv_ref[...]).astype(BF16)
    cos, sin = cos_ref[...], sin_ref[...]
    qa = _dot(nq, wqa_ref[...])
    qb = _dot(nq, wqb_ref[...])
    kn = _dot(nkv, wk_ref[...])
    vv = _dot(nkv, wv_ref[...])
    kr = pm[:, 384:512] * cos + pm[:, 512:640] * sin
    for h in range(MLA_HEADS):
        qn = qa[:, 256 * h:256 * h + 128]
        qr = qa[:, 256 * h + 128:256 * h + 256] * cos + qb[:, 128 * h:128 * h + 128] * sin
        q_ref[h] = (jnp.concatenate([qn, qr], axis=-1) * scale).astype(BF16)
        k_ref[h] = jnp.concatenate([kn[:, 128 * h:128 * h + 128], kr], axis=-1).astype(BF16)
        v_ref[h] = vv[:, 128 * h:128 * h + 128].astype(BF16)


def _mla_prep(pm, cos, sin, gq, gkv, wqa, wqb, wk, wv, scale, tm):
    tp = pm.shape[0]
    row = lambda w: pl.BlockSpec((tm, w), lambda i: (i, 0))
    head = lambda w: pl.BlockSpec((MLA_HEADS, tm, w), lambda i: (0, i, 0))
    return pl.pallas_call(
        functools.partial(_mla_prep_kernel, scale=scale),
        grid=(tp // tm,),
        in_specs=[row(pm.shape[1]), row(128), row(128), _full(gq.shape), _full(gkv.shape),
                  _full(wqa.shape), _full(wqb.shape), _full(wk.shape), _full(wv.shape)],
        out_specs=[head(256), head(256), head(128)],
        out_shape=[jax.ShapeDtypeStruct((MLA_HEADS, tp, 256), BF16),
                   jax.ShapeDtypeStruct((MLA_HEADS, tp, 256), BF16),
                   jax.ShapeDtypeStruct((MLA_HEADS, tp, 128), BF16)],
        compiler_params=_params("parallel"),
        name="mla_prep",
    )(pm, cos, sin, gq, gkv, wqa, wqb, wk, wv)


ATT_Q = 256


ATT_HEADS = 2


def _attn_kernel(q_ref, k_ref, v_ref, o_ref, *, n_q):
    heads = range(q_ref.shape[0])
    for qi in range(n_q):
        q0 = X0 + ATT_Q * qi
        kend = q0 + ATT_Q
        n = kend - K0
        kk = lax.broadcasted_iota(jnp.int32, (1, n), 1)
        ck = jnp.where(kk < 128 - N_META, 1 << 20, kk >> 6)
        cq = ((ATT_Q * qi + lax.broadcasted_iota(jnp.int32, (ATT_Q, 1), 0)) >> 6) + 2
        visible = ck <= cq
        s = [_dot_nt(q_ref[h, q0:q0 + ATT_Q, :], k_ref[h, K0:kend, :]) for h in heads]
        s = [jnp.where(visible, x, -1e30) for x in s]
        e = [jnp.exp(x - jnp.max(x, axis=-1, keepdims=True)) for x in s]
        l = [jnp.sum(x, axis=-1, keepdims=True) for x in e]
        o = [_dot(x.astype(BF16), v_ref[h, K0:kend, :]) for h, x in zip(heads, e)]
        for h in heads:
            o_ref[ATT_Q * qi:ATT_Q * (qi + 1), 128 * h:128 * (h + 1)] = o[h] / l[h]


def _attention(q, k, v, batch, lp, s_len):
    return pl.pallas_call(
        functools.partial(_attn_kernel, n_q=s_len // ATT_Q),
        grid=(batch, MLA_HEADS // ATT_HEADS),
        in_specs=[pl.BlockSpec((ATT_HEADS, lp, 256), lambda b, h: (h, b, 0)),
                  pl.BlockSpec((ATT_HEADS, lp, 256), lambda b, h: (h, b, 0)),
                  pl.BlockSpec((ATT_HEADS, lp, 128), lambda b, h: (h, b, 0))],
        out_specs=pl.BlockSpec((s_len, 128 * ATT_HEADS), lambda b, h: (b, h)),
        out_shape=jax.ShapeDtypeStruct((batch * s_len, MLA_HEADS * 128), F32),
        compiler_params=_params("parallel", "parallel"),
        name="mla_attn",
    )(q, k, v)


def _bdot(a, b):
    return _dot(a.astype(BF16), b.astype(BF16))


def _rwkv_scan_kernel(r_ref, lw_ref, k_ref, v_ref, kn_ref, b_ref, o_ref, z_ref):
    c = CHUNK
    w = 4 * c

    @pl.when(pl.program_id(1) == 0)
    def _():
        z_ref[...] = jnp.zeros_like(z_ref)

    ri = lax.broadcasted_iota(jnp.int32, (w, w), 0)
    ci = lax.broadcasted_iota(jnp.int32, (w, w), 1)
    same_head = (ri >> 6) == (ci >> 6)
    strict = (ri & 63) > (ci & 63)
    incl = (ri & 63) >= (ci & 63)
    eye = ri == ci
    tri = (lax.broadcasted_iota(jnp.int32, (c, c), 0) >= lax.broadcasted_iota(jnp.int32, (c, c), 1)).astype(F32)

    def stack(x):
        return jnp.where(same_head, jnp.concatenate([x, x, x, x], axis=0), 0.0)

    n_seq = lw_ref.shape[0]
    cum_all = [_dot(tri, lw_ref[p], HIGHEST) for p in range(n_seq)]
    groups = range(z_ref.shape[0])
    st = [dict() for _ in groups]
    for q in groups:
        p = q // 2
        lanes = slice(w * (q % 2), w * (q % 2) + w)
        lw = lw_ref[p, :, lanes]
        cum = cum_all[p][:, lanes]
        cum_prev = cum - lw
        mid = cum[c // 2 - 1:c // 2, :]
        tot = cum[c - 1:c, :]
        a = -kn_ref[p, :, lanes]
        b = b_ref[p, :, lanes]
        k = k_ref[p, :, lanes]
        r = r_ref[p, :, lanes]
        dn = jnp.exp(mid - cum)
        de = jnp.exp(tot - cum)
        a_t = stack(a * jnp.exp(cum_prev - mid))
        b_t = stack(b * dn)
        k_t = stack(k * dn)
        r_t = stack(r * jnp.exp(cum - mid))
        st[q].update(seq=p, lanes=lanes, tot=tot, a_0=stack(a * jnp.exp(cum_prev)), r_0=stack(r * jnp.exp(cum)),
                     b_e=stack(b * de), k_e=stack(k * de), vs=stack(v_ref[p, :, lanes]),
                     lhs=jnp.concatenate([a_t, r_t], axis=0).astype(BF16),
                     rhs=jnp.concatenate([b_t, k_t], axis=0).astype(BF16))
    for q in groups:
        inter = _dot_nt(st[q]["lhs"], st[q]["rhs"])
        l_ab = jnp.where(strict, inter[:w, :w], 0.0)
        st[q].update(l_ak=jnp.where(strict, inter[:w, w:], 0.0), m_rb=jnp.where(incl, inter[w:, :w], 0.0),
                     m_rk=jnp.where(incl, inter[w:, w:], 0.0),
                     t_inv=jnp.where(eye, 1.0, 0.0) + l_ab, l_ab=l_ab)
    for q in groups:
        st[q]["power"] = _bdot(st[q]["l_ab"], st[q]["l_ab"])
    for _ in range(int(math.log2(c)) - 2):
        for q in groups:
            both = _bdot(st[q]["power"], jnp.concatenate([st[q]["power"], st[q]["t_inv"]], axis=1))
            st[q]["power"], st[q]["t_inv"] = both[:, :w], st[q]["t_inv"] + both[:, w:]
    for q in groups:
        st[q]["t_inv"] = st[q]["t_inv"] + _bdot(st[q]["power"], st[q]["t_inv"])
    for q in groups:
        st[q]["pq"] = _bdot(st[q]["t_inv"], jnp.concatenate([st[q]["a_0"], st[q]["l_ak"]], axis=1))
    for q in groups:
        st[q]["mix"] = _bdot(jnp.concatenate([st[q]["m_rb"], st[q]["b_e"].T], axis=0), st[q]["pq"])
    for q in groups:
        mix, tot = st[q]["mix"], st[q]["tot"]
        r_hat = st[q]["r_0"] + mix[:w, :w]
        m_o = mix[:w, w:] + st[q]["m_rk"]
        g = jnp.where(eye, jnp.exp(tot), 0.0) + mix[w:, :w]
        m_h = mix[w:, w:] + st[q]["k_e"].T
        lhs = jnp.concatenate([jnp.concatenate([r_hat, m_o], axis=1), jnp.concatenate([g, m_h], axis=1)], axis=0)
        new = _bdot(lhs, jnp.concatenate([z_ref[q], st[q]["vs"]], axis=0))
        o_rows = new[:w]
        o_ref[st[q]["seq"], :, st[q]["lanes"]] = o_rows[0:c] + o_rows[c:2 * c] + o_rows[2 * c:3 * c] + o_rows[3 * c:4 * c]
        z_ref[q] = new[w:]


SCAN_SEQS = 2


def _rwkv_scan(r, lw, k, v, kn, b, batch, lp, s_len):
    nc = s_len // CHUNK + 1
    assert batch % SCAN_SEQS == 0
    pairs = batch // SCAN_SEQS
    ins = [a.reshape(pairs, SCAN_SEQS, lp, 512) for a in (r, lw, k, v, kn, b)]
    blk = pl.BlockSpec((None, SCAN_SEQS, CHUNK, 512), lambda bi, ci: (bi, 0, SCAN_C0 + ci, 0))
    out = pl.pallas_call(
        _rwkv_scan_kernel,
        grid=(pairs, nc),
        in_specs=[blk] * 6,
        out_specs=pl.BlockSpec((None, SCAN_SEQS, CHUNK, 512), lambda bi, ci: (bi, 0, jnp.maximum(ci - 1, 0), 0)),
        out_shape=jax.ShapeDtypeStruct((pairs, SCAN_SEQS, s_len, 512), F32),
        scratch_shapes=[pltpu.VMEM((2 * SCAN_SEQS, 256, 256), F32)],
        compiler_params=_params("parallel", "arbitrary"),
        name="rwkv_scan",
    )(*ins)
    return out.reshape(batch * s_len, 512)


def _outproj_kernel(x_ref, om_ref, orw_ref, bonus_ref, g_ref, og_ref, lnw_ref, lnb_ref, bd_ref,
                    wout_ref, gffn_ref, wq_ref, h2_ref, t_ref, q_ref):
    y_mla = _rms(om_ref[...], og_ref[...])
    o = orw_ref[...]
    bd = bd_ref[...]
    mean = _dot_split(o, bd) * (1.0 / RWKV_HEAD)
    cen = o - mean
    var = _dot_split(cen * cen, bd) * (1.0 / RWKV_HEAD)
    y_rwkv = (cen * lax.rsqrt(var + GN_EPS) * lnw_ref[...] + lnb_ref[...] + bonus_ref[...]) * g_ref[...]
    y = jnp.concatenate([y_mla, y_rwkv], axis=-1).astype(BF16)
    h2 = x_ref[...] + _dot(y, wout_ref[...])
    h2_ref[...] = h2
    t = _rms(h2, gffn_ref[...]).astype(BF16)
    t_ref[...] = t
    q_ref[...] = _dot(t, wq_ref[...])


def _outproj(x, o_mla, o_rwkv, bonus, g, og, lnw, lnb, bd, wout, gffn, wq, batch, lp, s_len, tm):
    t_all, d = x.shape
    ns = s_len // tm
    npad = lp // tm
    xrow = lambda w: pl.BlockSpec((tm, w), lambda b, j: (b * ns + j, 0))
    prow = lambda w: pl.BlockSpec((tm, w), lambda b, j: (b * npad + X0 // tm + j, 0))
    vec = lambda a: _full(a.shape)
    return pl.pallas_call(
        _outproj_kernel,
        grid=(batch, ns),
        in_specs=[xrow(d), xrow(512), xrow(512), prow(512), prow(512), vec(og), vec(lnw), vec(lnb), vec(bd),
                  vec(wout), vec(gffn), vec(wq)],
        out_specs=[xrow(d), xrow(d), xrow(wq.shape[1])],
        out_shape=[jax.ShapeDtypeStruct((t_all, d), F32), jax.ShapeDtypeStruct((t_all, d), BF16),
                   jax.ShapeDtypeStruct((t_all, wq.shape[1]), F32)],
        compiler_params=_params("parallel", "parallel"),
        name="outproj",
    )(x, o_mla, o_rwkv, bonus, g, og, lnw, lnb, bd, wout, gffn, wq)


_CANDS = [(i, j) for i in range(PEER_TOPK) for j in range(PEER_TOPK) if (i + 1) * (j + 1) <= PEER_TOPK]


def _exchange(v, i, l):
    v[i], v[l] = jnp.maximum(v[i], v[l]), jnp.minimum(v[i], v[l])


def _bitonic_merge_desc(v):
    n = len(v)
    j = n // 2
    while j >= 1:
        for i in range(n):
            if i ^ j > i:
                _exchange(v, i, i ^ j)
        j //= 2
    return v


def _bitonic_sort_desc(v):
    n = len(v)
    k = 2
    while k <= n:
        j = k // 2
        while j >= 1:
            for i in range(n):
                l = i ^ j
                if l > i:
                    if (i & k) == 0 or k == n:
                        _exchange(v, i, l)
                    else:
                        _exchange(v, l, i)
            j //= 2
        k *= 2
    return v


def _top16_rows(s):
    v = _bitonic_sort_desc([s[8 * k:8 * k + 8, :] for k in range(PEER_TOPK)])
    for shift in (4, 2, 1):
        other = [pltpu.roll(x, shift, 0) for x in v]
        v = _bitonic_merge_desc([jnp.maximum(v[i], other[PEER_TOPK - 1 - i]) for i in range(PEER_TOPK)])
    return v


def _top_distinct(x, weight, n):
    out = []
    for _ in range(n):
        m = jnp.max(x, axis=0, keepdims=True)
        eq = x == m
        cnt = jnp.sum(jnp.where(eq, weight, 0.0), axis=0, keepdims=True)
        x = jnp.where(eq, -jnp.inf, x)
        out.append((m, cnt))
    return out


def _route_kernel(q_ref, keys_ref, th_ref, s2_ref, e1_ref, e2_ref, *, heads, tm):
    qb = q_ref[...].astype(BF16)
    pad = 56 - len(_CANDS)
    neg = jnp.full((pad, tm), -jnp.inf, F32)
    cw = jnp.concatenate([jnp.ones((len(_CANDS), tm), F32), jnp.zeros((pad, tm), F32)], axis=0)
    for h in range(heads):
        s = [_dot_nt(keys_ref[2 * h + c], qb[:, 128 * (2 * h + c):128 * (2 * h + c + 1)]) for c in range(2)]
        top = [[x[0:1, :] for x in _top16_rows(s[c])] for c in range(2)]
        cv = jnp.concatenate([top[0][i] + top[1][j] for i, j in _CANDS] + [neg], axis=0)
        tau = jnp.full((1, tm), -jnp.inf, F32)
        seen = jnp.zeros((1, tm), F32)
        for m, cnt in _top_distinct(cv, cw, PEER_TOPK):
            seen = seen + cnt
            tau = jnp.maximum(tau, jnp.where(seen >= PEER_TOPK, m, -jnp.inf))
        m1, m2 = top[0][0], top[1][0]
        z = jnp.sum(jnp.where(cv >= tau, cw * jnp.exp(cv - (m1 + m2)), 0.0), axis=0, keepdims=True)
        e1 = jnp.exp(s[0] - m1) / z
        e2 = jnp.exp(s[1] - m2)
        second = jnp.concatenate(top[1], axis=0)
        theta = [jnp.min(jnp.where(second + top[0][i] >= tau, second, jnp.inf), axis=0, keepdims=True)
                 for i in range(PEER_TOPK)]
        th = jnp.full(s[0].shape, jnp.inf, F32)
        for i in range(PEER_TOPK):
            th = jnp.minimum(th, jnp.where(s[0] >= top[0][i], theta[i], jnp.inf))
        for lb in range(tm // 128):
            lanes = slice(128 * lb, 128 * lb + 128)
            th_ref[lb, h] = th[:, lanes]
            s2_ref[lb, h] = s[1][:, lanes]
            e1_ref[lb, h] = e1[:, lanes]
            e2_ref[lb, h] = e2[:, lanes]


def _route(q, keys, heads, tm):
    t_all = q.shape[0]
    big = pl.BlockSpec((tm // 128, heads, 128, 128), lambda i: (i, 0, 0, 0))
    return pl.pallas_call(
        functools.partial(_route_kernel, heads=heads, tm=tm),
        grid=(t_all // tm,),
        in_specs=[pl.BlockSpec((tm, q.shape[1]), lambda i: (i, 0)), _full(keys.shape)],
        out_specs=[big] * 4,
        out_shape=[jax.ShapeDtypeStruct((t_all // 128, heads, 128, 128), F32)] * 4,
        compiler_params=_params("parallel"),
        name="peer_route",
    )(q, keys)


GATE_GROUP = 2


def _peer_kernel(t_ref, u_ref, vt_ref, th_ref, s2_ref, e1_ref, e2_ref, h2_ref, gf_ref,
                 y_ref, acc_ref, gate_ref, *, heads, te):
    j = pl.program_id(1)
    tm = t_ref.shape[0]

    @pl.when(j == 0)
    def _():
        acc_ref[...] = jnp.zeros_like(acc_ref)

    tile = (8, 128)
    n_a = te // 128
    for a8 in range(0, n_a, 8):
        a_rows = pl.ds(pl.multiple_of(j * n_a + a8, 8), 8)
        for ag in range(0, 8, GATE_GROUP):

            def lane_block(lb, carry, a_rows=a_rows, ag=ag, a8=a8):
                acc = [[jnp.zeros(tile, F32) for _ in range(16)] for _ in range(GATE_GROUP)]
                for h in range(heads):
                    th = th_ref[lb, h, a_rows, :]
                    e1 = e1_ref[lb, h, a_rows, :]
                    thb = [jnp.broadcast_to(th[ag + ai:ag + ai + 1, :], tile) for ai in range(GATE_GROUP)]
                    e1b = [jnp.broadcast_to(e1[ag + ai:ag + ai + 1, :], tile) for ai in range(GATE_GROUP)]
                    for r in range(16):
                        s2 = s2_ref[lb, h, 8 * r:8 * r + 8, :]
                        e2 = e2_ref[lb, h, 8 * r:8 * r + 8, :]
                        for ai in range(GATE_GROUP):
                            acc[ai][r] = acc[ai][r] + jnp.where(s2 >= thb[ai], e2, 0.0) * e1b[ai]
                for ai in range(GATE_GROUP):
                    for r in range(16):
                        row = (a8 + ag + ai) * 128 + 8 * r
                        gate_ref[lb, row:row + 8, :] = acc[ai][r]
                return carry

            lax.fori_loop(0, tm // 128, lane_block, 0)

    hp = _dot_nt(u_ref[...], t_ref[...])
    gate = jnp.concatenate([gate_ref[lb] for lb in range(tm // 128)], axis=1)
    act = 0.5 * hp * (1.0 + lax.erf(hp * (1.0 / math.sqrt(2.0)))) * gate
    acc_ref[...] += _dot(vt_ref[...], act.astype(BF16))

    @pl.when(j == pl.num_programs(1) - 1)
    def _():
        y_ref[...] = _rms(h2_ref[...] + acc_ref[...].T, gf_ref[...])


def _peer(t, u, vt, th, s2, e1, e2, h2, gf, heads, tm, te):
    t_all, d = t.shape
    n_exp = u.shape[0]
    tok = pl.BlockSpec((tm // 128, heads, 128, 128), lambda i, j: (i, 0, 0, 0))
    return pl.pallas_call(
        functools.partial(_peer_kernel, heads=heads, te=te),
        grid=(t_all // tm, n_exp // te),
        in_specs=[pl.BlockSpec((tm, d), lambda i, j: (i, 0)),
                  pl.BlockSpec((te, d), lambda i, j: (j, 0)),
                  pl.BlockSpec((d, te), lambda i, j: (0, j)),
                  tok, tok, tok, tok,
                  pl.BlockSpec((tm, d), lambda i, j: (i, 0)),
                  _full(gf.shape)],
        out_specs=pl.BlockSpec((tm, d), lambda i, j: (i, 0)),
        out_shape=jax.ShapeDtypeStruct((t_all, d), F32),
        scratch_shapes=[pltpu.VMEM((d, tm), F32), pltpu.VMEM((tm // 128, te, 128), F32)],
        compiler_params=_params("parallel", "arbitrary"),
        name="peer_experts",
    )(t, u, vt, th, s2, e1, e2, h2, gf)


def _tile(n, cap):
    t = cap
    while n % t:
        t //= 2
    return t


def kernel(x, meta_tokens, norm_mix_g, w_in, mla_q_norm_g, mla_w_uq, mla_kv_norm_g, mla_w_ukv, mla_out_g, rwkv_mu, rwkv_w0, rwkv_w_up, rwkv_a0, rwkv_a_up, rwkv_g_up, rwkv_k_k, rwkv_k_a, rwkv_r_k, rwkv_ln_w, rwkv_ln_b, w_out, norm_ffn_g, peer_w_q, peer_sub_keys, peer_u, peer_v, norm_final_g):
    batch, s_len, d = x.shape
    assert w_in.shape[0] == 1, "one layer"
    assert s_len % ATT_Q == 0 and d == 1024
    lp = X0 + s_len
    tp = batch * lp
    row = lambda a: a.reshape(1, -1).astype(F32)

    head = jnp.concatenate([jnp.zeros((META0, d), F32), meta_tokens.astype(F32)], axis=0)

    wi = w_in[0]
    kr = wi[:, 384:448]
    z64 = jnp.zeros((d, 64), F32)
    w_ext = jnp.concatenate([wi[:, 0:384], kr, z64, kr[:, 32:], kr[:, :32], z64, wi[:, 448:]], axis=1).astype(BF16)
    n_mla = 640
    z_up = jnp.zeros((64, 512), F32)
    wup = jnp.concatenate([rwkv_w_up[0], z_up], axis=0).astype(BF16)
    aup = jnp.concatenate([z_up, rwkv_a_up[0]], axis=0).astype(BF16)
    head_id = jnp.arange(512) // RWKV_HEAD
    bd = (head_id[:, None] == head_id[None, :]).astype(BF16)
    rwkv = (row(rwkv_mu[0]), row(rwkv_w0[0]), row(rwkv_a0[0]), row(rwkv_k_k[0]), row(rwkv_k_a[0]),
            row(rwkv_r_k[0]), wup, aup, rwkv_g_up[0].astype(BF16), bd)
    pm, r, lw, k2, vv, kn, bv, bonus, g = _inproj(head, x, row(norm_mix_g[0]), w_ext, rwkv, n_mla, _tile(s_len, 512))

    pos = jnp.maximum(jnp.arange(lp) - META0, 0).astype(F32)
    inv = ROPE_THETA ** (-jnp.arange(32, dtype=F32) / 32)
    ang = pos[:, None] * inv[None, :]
    zl = jnp.zeros((lp, 64), F32)
    cos = jnp.tile(jnp.concatenate([jnp.cos(ang), jnp.cos(ang), zl], axis=1), (batch, 1))
    sin = jnp.tile(jnp.concatenate([-jnp.sin(ang), jnp.sin(ang), zl], axis=1), (batch, 1))

    wuq = mla_w_uq[0].reshape(256, MLA_HEADS, 192)
    zq = jnp.zeros((256, MLA_HEADS, 64), F32)
    wqa = jnp.concatenate([wuq, zq], axis=2).reshape(256, MLA_HEADS * 256).astype(BF16)
    wqb = jnp.concatenate([wuq[:, :, 160:192], wuq[:, :, 128:160], zq], axis=2).reshape(256, MLA_HEADS * 128).astype(BF16)
    wukv = mla_w_ukv[0].reshape(128, MLA_HEADS, 256)
    wk = wukv[:, :, :128].reshape(128, MLA_HEADS * 128).astype(BF16)
    wv = wukv[:, :, 128:].reshape(128, MLA_HEADS * 128).astype(BF16)
    q, k, v = _mla_prep(pm, cos, sin, row(mla_q_norm_g[0]), row(mla_kv_norm_g[0]), wqa, wqb, wk, wv,
                        192 ** -0.5, _tile(tp, 512))
    o_mla = _attention(q, k, v, batch, lp, s_len)

    o_rwkv = _rwkv_scan(r, lw, k2, vv, kn, bv, batch, lp, s_len)

    heads = peer_sub_keys.shape[1]
    h2, t, pq = _outproj(x.reshape(batch * s_len, d), o_mla, o_rwkv, bonus, g, row(mla_out_g[0]),
                         row(rwkv_ln_w[0]), row(rwkv_ln_b[0]), bd, w_out[0].astype(BF16), row(norm_ffn_g[0]),
                         peer_w_q[0].astype(BF16), batch, lp, s_len, _tile(s_len, 512))

    t_all = batch * s_len
    keys = peer_sub_keys[0].reshape(heads * 2, 128, 128).astype(BF16)
    tm = _tile(t_all, 512)
    th, s2, e1, e2 = _route(pq, keys, heads, tm)
    y = _peer(t, peer_u[0].astype(BF16), peer_v[0].T.astype(BF16), th, s2, e1, e2, h2,
              row(norm_final_g), heads, tm, 1024)
    return y.reshape(batch, s_len, d)
```

```python
import functools
import math

import jax
import jax.numpy as jnp
from jax import lax
from jax.experimental import pallas as pl
from jax.experimental.pallas import tpu as pltpu

F32 = jnp.float32
BF16 = jnp.bfloat16
HIGHEST = lax.Precision.HIGHEST

RMS_EPS = 1e-6
GN_EPS = 64e-5
ROPE_THETA = 10000.0
N_META = 16
CHUNK = 64
X0 = 512
META0 = X0 - N_META
K0 = X0 - 128
SCAN_C0 = (X0 - CHUNK) // CHUNK
MLA_HEADS = 4
RWKV_HEAD = 64
PEER_TOPK = 16
VMEM_LIMIT = 56 * 1024 * 1024


def _dot(a, b, precision=None):
    return jnp.dot(a, b, preferred_element_type=F32, precision=precision)


def _dot_nt(a, b, precision=None):
    return lax.dot_general(a, b, (((1,), (1,)), ((), ())),
                           preferred_element_type=F32, precision=precision)


def _dot_split(x, m):
    hi = x.astype(BF16)
    lo = (x - hi.astype(F32)).astype(BF16)
    return _dot(hi, m) + _dot(lo, m)


def _rms(x, g):
    return x * lax.rsqrt(jnp.mean(x * x, axis=-1, keepdims=True) + RMS_EPS) * g


def _params(*sem):
    return pltpu.CompilerParams(dimension_semantics=sem, vmem_limit_bytes=VMEM_LIMIT)


def _full(shape):
    n = len(shape)
    return pl.BlockSpec(shape, lambda *_: (0,) * n)


def _softplus(x):
    return jnp.maximum(x, 0.0) + jnp.log(1.0 + jnp.exp(-jnp.abs(x)))


def _sigmoid(x):
    return 1.0 / (1.0 + jnp.exp(-x))


def _rwkv_prep(p, prev_ref, row0, params, outs):
    mu_ref, w0_ref, a0_ref, kk_ref, ka_ref, rk_ref, wup_ref, aup_ref, gup_ref, bd_ref = params
    r_ref, lw_ref, k_ref, v_ref, kn_ref, b_ref, bonus_ref, g_ref = outs
    tm = p.shape[0]
    rows = lax.broadcasted_iota(jnp.int32, (tm, 1), 0)
    shifted = jnp.where(rows == 0, prev_ref[7:8, :], pltpu.roll(p, 1, 0))
    prev_ref[...] = p[tm - 8:tm, :]
    pm = p + mu_ref[...] * (shifted - p)
    r, k, v = pm[:, 0:512], pm[:, 512:1024], pm[:, 1024:1536]
    dd, dg = pm[:, 1536:1664], pm[:, 1664:1792]
    w = -_softplus(-(w0_ref[...] + _dot(jnp.tanh(dd).astype(BF16), wup_ref[...]))) - 0.5
    log_decay = -jnp.exp(w)
    a = _sigmoid(a0_ref[...] + _dot(dd.astype(BF16), aup_ref[...]))
    g_ref[...] = _dot(_sigmoid(dg).astype(BF16), gup_ref[...])
    bd = bd_ref[...]
    kk = k * kk_ref[...]
    norm = jnp.sqrt(_dot_split(kk * kk, bd))
    kk = kk / jnp.maximum(norm, 1e-12)
    k2 = k * (1.0 + (a - 1.0) * ka_ref[...])
    real = (row0 + rows) >= META0
    r_ref[...] = r
    lw_ref[...] = jnp.where(real, log_decay, 0.0)
    k_ref[...] = k2
    v_ref[...] = v
    kn_ref[...] = kk
    b_ref[...] = kk * a
    bonus_ref[...] = _dot_split(r * k2 * rk_ref[...], bd) * v


def _inproj_kernel(head_ref, x_ref, g_ref, w_ref, mu_ref, w0_ref, a0_ref, kk_ref, ka_ref, rk_ref,
                   wup_ref, aup_ref, gup_ref, bd_ref,
                   pm_ref, r_ref, lw_ref, k_ref, v_ref, kn_ref, b_ref, bonus_ref, gate_ref, prev_ref,
                   *, n_mla, n_head, tm):
    j = pl.program_id(1)

    @pl.when(j == 0)
    def _():
        prev_ref[...] = jnp.zeros_like(prev_ref)

    def project(rows):
        n = _rms(rows, g_ref[...]).astype(BF16)
        p = _dot(n, w_ref[...])
        pm_ref[...] = p[:, :n_mla]
        _rwkv_prep(p[:, n_mla:], prev_ref, j * tm,
                   (mu_ref, w0_ref, a0_ref, kk_ref, ka_ref, rk_ref, wup_ref, aup_ref, gup_ref, bd_ref),
                   (r_ref, lw_ref, k_ref, v_ref, kn_ref, b_ref, bonus_ref, gate_ref))

    @pl.when(j < n_head)
    def _():
        project(head_ref[...])

    @pl.when(j >= n_head)
    def _():
        project(x_ref[...])


def _inproj(head, x, g, w, rwkv, n_mla, tm):
    batch, s_len, d = x.shape
    n_head = head.shape[0] // tm
    nl = n_head + s_len // tm
    rows = batch * nl * tm
    out = lambda width: pl.BlockSpec((tm, width), lambda b, j: (b * nl + j, 0))
    return pl.pallas_call(
        functools.partial(_inproj_kernel, n_mla=n_mla, n_head=n_head, tm=tm),
        grid=(batch, nl),
        in_specs=[pl.BlockSpec((tm, d), lambda b, j: (jnp.minimum(j, n_head - 1), 0)),
                  pl.BlockSpec((None, tm, d), lambda b, j: (b, jnp.maximum(j - n_head, 0), 0)),
                  _full((1, d)), _full(w.shape)] + [_full(a.shape) for a in rwkv],
        out_specs=[out(n_mla)] + [out(512)] * 8,
        out_shape=[jax.ShapeDtypeStruct((rows, n_mla), F32)] + [jax.ShapeDtypeStruct((rows, 512), F32)] * 8,
        scratch_shapes=[pltpu.VMEM((8, w.shape[1] - n_mla), F32)],
        compiler_params=_params("parallel", "arbitrary"),
        name="inproj",
    )(head, x, g, w, *rwkv)


def _mla_prep_kernel(pm_ref, cos_ref, sin_ref, gq_ref, gkv_ref, wqa_ref, wqb_ref, wk_ref, wv_ref,
                     q_ref, k_ref, v_ref, *, scale):
    pm = pm_ref[...]
    nq = _rms(pm[:, 0:256], gq_ref[...]).astype(BF16)
    nkv = _rms(pm[:, 256:384], gkv_ref[...]).astype(BF16)
    cos, sin = cos_ref[...], sin_ref[...]
    qa = _dot(nq, wqa_ref[...])
    qb = _dot(nq, wqb_ref[...])
    kn = _dot(nkv, wk_ref[...])
    vv = _dot(nkv, wv_ref[...])
    kr = pm[:, 384:512] * cos + pm[:, 512:640] * sin
    for h in range(MLA_HEADS):
        qn = qa[:, 256 * h:256 * h + 128]
        qr = qa[:, 256 * h + 128:256 * h + 256] * cos + qb[:, 128 * h:128 * h + 128] * sin
        q_ref[h] = (jnp.concatenate([qn, qr], axis=-1) * scale).astype(BF16)
        k_ref[h] = jnp.concatenate([kn[:, 128 * h:128 * h + 128], kr], axis=-1).astype(BF16)
        v_ref[h] = vv[:, 128 * h:128 * h + 128].astype(BF16)


def _mla_prep(pm, cos, sin, gq, gkv, wqa, wqb, wk, wv, scale, tm):
    tp = pm.shape[0]
    row = lambda w: pl.BlockSpec((tm, w), lambda i: (i, 0))
    head = lambda w: pl.BlockSpec((MLA_HEADS, tm, w), lambda i: (0, i, 0))
    return pl.pallas_call(
        functools.partial(_mla_prep_kernel, scale=scale),
        grid=(tp // tm,),
        in_specs=[row(pm.shape[1]), row(128), row(128), _full(gq.shape), _full(gkv.shape),
                  _full(wqa.shape), _full(wqb.shape), _full(wk.shape), _full(wv.shape)],
        out_specs=[head(256), head(256), head(128)],
        out_shape=[jax.ShapeDtypeStruct((MLA_HEADS, tp, 256), BF16),
                   jax.ShapeDtypeStruct((MLA_HEADS, tp, 256), BF16),
                   jax.ShapeDtypeStruct((MLA_HEADS, tp, 128), BF16)],
        compiler_params=_params("parallel"),
        name="mla_prep",
    )(pm, cos, sin, gq, gkv, wqa, wqb, wk, wv)


ATT_Q = 256


ATT_HEADS = 2


def _attn_kernel(q_ref, k_ref, v_ref, o_ref, *, n_q):
    heads = range(q_ref.shape[0])
    for qi in range(n_q):
        q0 = X0 + ATT_Q * qi
        kend = q0 + ATT_Q
        n = kend - K0
        kk = lax.broadcasted_iota(jnp.int32, (1, n), 1)
        ck = jnp.where(kk < 128 - N_META, 1 << 20, kk >> 6)
        cq = ((ATT_Q * qi + lax.broadcasted_iota(jnp.int32, (ATT_Q, 1), 0)) >> 6) + 2
        visible = ck <= cq
        s = [_dot_nt(q_ref[h, q0:q0 + ATT_Q, :], k_ref[h, K0:kend, :]) for h in heads]
        s = [jnp.where(visible, x, -1e30) for x in s]
        e = [jnp.exp(x - jnp.max(x, axis=-1, keepdims=True)) for x in s]
        l = [jnp.sum(x, axis=-1, keepdims=True) for x in e]
        o = [_dot(x.astype(BF16), v_ref[h, K0:kend, :]) for h, x in zip(heads, e)]
        for h in heads:
            o_ref[ATT_Q * qi:ATT_Q * (qi + 1), 128 * h:128 * (h + 1)] = o[h] / l[h]


def _attention(q, k, v, batch, lp, s_len):
    return pl.pallas_call(
        functools.partial(_attn_kernel, n_q=s_len // ATT_Q),
        grid=(batch, MLA_HEADS // ATT_HEADS),
        in_specs=[pl.BlockSpec((ATT_HEADS, lp, 256), lambda b, h: (h, b, 0)),
                  pl.BlockSpec((ATT_HEADS, lp, 256), lambda b, h: (h, b, 0)),
                  pl.BlockSpec((ATT_HEADS, lp, 128), lambda b, h: (h, b, 0))],
        out_specs=pl.BlockSpec((s_len, 128 * ATT_HEADS), lambda b, h: (b, h)),
        out_shape=jax.ShapeDtypeStruct((batch * s_len, MLA_HEADS * 128), F32),
        compiler_params=_params("parallel", "parallel"),
        name="mla_attn",
    )(q, k, v)


def _bdot(a, b):
    return _dot(a.astype(BF16), b.astype(BF16))


def _rwkv_scan_kernel(r_ref, lw_ref, k_ref, v_ref, kn_ref, b_ref, o_ref, z_ref):
    c = CHUNK
    w = 4 * c

    @pl.when(pl.program_id(1) == 0)
    def _():
        z_ref[...] = jnp.zeros_like(z_ref)

    ri = lax.broadcasted_iota(jnp.int32, (w, w), 0)
    ci = lax.broadcasted_iota(jnp.int32, (w, w), 1)
    same_head = (ri >> 6) == (ci >> 6)
    strict = (ri & 63) > (ci & 63)
    incl = (ri & 63) >= (ci & 63)
    eye = ri == ci
    tri = (lax.broadcasted_iota(jnp.int32, (c, c), 0) >= lax.broadcasted_iota(jnp.int32, (c, c), 1)).astype(F32)

    def stack(x):
        return jnp.where(same_head, jnp.concatenate([x, x, x, x], axis=0), 0.0)

    n_seq = lw_ref.shape[0]
    cum_all = [_dot(tri, lw_ref[p], HIGHEST) for p in range(n_seq)]
    groups = range(z_ref.shape[0])
    st = [dict() for _ in groups]
    for q in groups:
        p = q // 2
        lanes = slice(w * (q % 2), w * (q % 2) + w)
        lw = lw_ref[p, :, lanes]
        cum = cum_all[p][:, lanes]
        cum_prev = cum - lw
        mid = cum[c // 2 - 1:c // 2, :]
        tot = cum[c - 1:c, :]
        a = -kn_ref[p, :, lanes]
        b = b_ref[p, :, lanes]
        k = k_ref[p, :, lanes]
        r = r_ref[p, :, lanes]
        dn = jnp.exp(mid - cum)
        de = jnp.exp(tot - cum)
        a_t = stack(a * jnp.exp(cum_prev - mid))
        b_t = stack(b * dn)
        k_t = stack(k * dn)
        r_t = stack(r * jnp.exp(cum - mid))
        st[q].update(seq=p, lanes=lanes, tot=tot, a_0=stack(a * jnp.exp(cum_prev)), r_0=stack(r * jnp.exp(cum)),
                     b_e=stack(b * de), k_e=stack(k * de), vs=stack(v_ref[p, :, lanes]),
                     lhs=jnp.concatenate([a_t, r_t], axis=0).astype(BF16),
                     rhs=jnp.concatenate([b_t, k_t], axis=0).astype(BF16))
    for q in groups:
        inter = _dot_nt(st[q]["lhs"], st[q]["rhs"])
        l_ab = jnp.where(strict, inter[:w, :w], 0.0)
        st[q].update(l_ak=jnp.where(strict, inter[:w, w:], 0.0), m_rb=jnp.where(incl, inter[w:, :w], 0.0),
                     m_rk=jnp.where(incl, inter[w:, w:], 0.0),
                     t_inv=jnp.where(eye, 1.0, 0.0) + l_ab, l_ab=l_ab)
    for q in groups:
        st[q]["power"] = _bdot(st[q]["l_ab"], st[q]["l_ab"])
    for _ in range(int(math.log2(c)) - 2):
        for q in groups:
            both = _bdot(st[q]["power"], jnp.concatenate([st[q]["power"], st[q]["t_inv"]], axis=1))
            st[q]["power"], st[q]["t_inv"] = both[:, :w], st[q]["t_inv"] + both[:, w:]
    for q in groups:
        st[q]["t_inv"] = st[q]["t_inv"] + _bdot(st[q]["power"], st[q]["t_inv"])
    for q in groups:
        st[q]["pq"] = _bdot(st[q]["t_inv"], jnp.concatenate([st[q]["a_0"], st[q]["l_ak"]], axis=1))
    for q in groups:
        st[q]["mix"] = _bdot(jnp.concatenate([st[q]["m_rb"], st[q]["b_e"].T], axis=0), st[q]["pq"])
    for q in groups:
        mix, tot = st[q]["mix"], st[q]["tot"]
        r_hat = st[q]["r_0"] + mix[:w, :w]
        m_o = mix[:w, w:] + st[q]["m_rk"]
        g = jnp.where(eye, jnp.exp(tot), 0.0) + mix[w:, :w]
        m_h = mix[w:, w:] + st[q]["k_e"].T
        lhs = jnp.concatenate([jnp.concatenate([r_hat, m_o], axis=1), jnp.concatenate([g, m_h], axis=1)], axis=0)
        new = _bdot(lhs, jnp.concatenate([z_ref[q], st[q]["vs"]], axis=0))
        o_rows = new[:w]
        o_ref[st[q]["seq"], :, st[q]["lanes"]] = o_rows[0:c] + o_rows[c:2 * c] + o_rows[2 * c:3 * c] + o_rows[3 * c:4 * c]
        z_ref[q] = new[w:]


SCAN_SEQS = 2


def _rwkv_scan(r, lw, k, v, kn, b, batch, lp, s_len):
    nc = s_len // CHUNK + 1
    assert batch % SCAN_SEQS == 0
    pairs = batch // SCAN_SEQS
    ins = [a.reshape(pairs, SCAN_SEQS, lp, 512) for a in (r, lw, k, v, kn, b)]
    blk = pl.BlockSpec((None, SCAN_SEQS, CHUNK, 512), lambda bi, ci: (bi, 0, SCAN_C0 + ci, 0))
    out = pl.pallas_call(
        _rwkv_scan_kernel,
        grid=(pairs, nc),
        in_specs=[blk] * 6,
        out_specs=pl.BlockSpec((None, SCAN_SEQS, CHUNK, 512), lambda bi, ci: (bi, 0, jnp.maximum(ci - 1, 0), 0)),
        out_shape=jax.ShapeDtypeStruct((pairs, SCAN_SEQS, s_len, 512), F32),
        scratch_shapes=[pltpu.VMEM((2 * SCAN_SEQS, 256, 256), F32)],
        compiler_params=_params("parallel", "arbitrary"),
        name="rwkv_scan",
    )(*ins)
    return out.reshape(batch * s_len, 512)


def _outproj_kernel(x_ref, om_ref, orw_ref, bonus_ref, g_ref, og_ref, lnw_ref, lnb_ref, bd_ref,
                    wout_ref, gffn_ref, wq_ref, h2_ref, t_ref, q_ref):
    y_mla = _rms(om_ref[...], og_ref[...])
    o = orw_ref[...]
    bd = bd_ref[...]
    mean = _dot_split(o, bd) * (1.0 / RWKV_HEAD)
    cen = o - mean
    var = _dot_split(cen * cen, bd) * (1.0 / RWKV_HEAD)
    y_rwkv = (cen * lax.rsqrt(var + GN_EPS) * lnw_ref[...] + lnb_ref[...] + bonus_ref[...]) * g_ref[...]
    y = jnp.concatenate([y_mla, y_rwkv], axis=-1).astype(BF16)
    h2 = x_ref[...] + _dot(y, wout_ref[...])
    h2_ref[...] = h2
    t = _rms(h2, gffn_ref[...]).astype(BF16)
    t_ref[...] = t
    q_ref[...] = _dot(t, wq_ref[...])


def _outproj(x, o_mla, o_rwkv, bonus, g, og, lnw, lnb, bd, wout, gffn, wq, batch, lp, s_len, tm):
    t_all, d = x.shape
    ns = s_len // tm
    npad = lp // tm
    xrow = lambda w: pl.BlockSpec((tm, w), lambda b, j: (b * ns + j, 0))
    prow = lambda w: pl.BlockSpec((tm, w), lambda b, j: (b * npad + X0 // tm + j, 0))
    vec = lambda a: _full(a.shape)
    return pl.pallas_call(
        _outproj_kernel,
        grid=(batch, ns),
        in_specs=[xrow(d), xrow(512), xrow(512), prow(512), prow(512), vec(og), vec(lnw), vec(lnb), vec(bd),
                  vec(wout), vec(gffn), vec(wq)],
        out_specs=[xrow(d), xrow(d), xrow(wq.shape[1])],
        out_shape=[jax.ShapeDtypeStruct((t_all, d), F32), jax.ShapeDtypeStruct((t_all, d), BF16),
                   jax.ShapeDtypeStruct((t_all, wq.shape[1]), F32)],
        compiler_params=_params("parallel", "parallel"),
        name="outproj",
    )(x, o_mla, o_rwkv, bonus, g, og, lnw, lnb, bd, wout, gffn, wq)


_CANDS = [(i, j) for i in range(PEER_TOPK) for j in range(PEER_TOPK) if (i + 1) * (j + 1) <= PEER_TOPK]


def _exchange(v, i, l):
    v[i], v[l] = jnp.maximum(v[i], v[l]), jnp.minimum(v[i], v[l])


def _bitonic_merge_desc(v):
    n = len(v)
    j = n // 2
    while j >= 1:
        for i in range(n):
            if i ^ j > i:
                _exchange(v, i, i ^ j)
        j //= 2
    return v


def _bitonic_sort_desc(v):
    n = len(v)
    k = 2
    while k <= n:
        j = k // 2
        while j >= 1:
            for i in range(n):
                l = i ^ j
                if l > i:
                    if (i & k) == 0 or k == n:
                        _exchange(v, i, l)
                    else:
                        _exchange(v, l, i)
            j //= 2
        k *= 2
    return v


def _top16_rows(s):
    v = _bitonic_sort_desc([s[8 * k:8 * k + 8, :] for k in range(PEER_TOPK)])
    for shift in (4, 2, 1):
        other = [pltpu.roll(x, shift, 0) for x in v]
        v = _bitonic_merge_desc([jnp.maximum(v[i], other[PEER_TOPK - 1 - i]) for i in range(PEER_TOPK)])
    return v


def _top_distinct(x, weight, n):
    out = []
    for _ in range(n):
        m = jnp.max(x, axis=0, keepdims=True)
        eq = x == m
        cnt = jnp.sum(jnp.where(eq, weight, 0.0), axis=0, keepdims=True)
        x = jnp.where(eq, -jnp.inf, x)
        out.append((m, cnt))
    return out


def _route_kernel(q_ref, keys_ref, s1_ref, t1_ref, th_ref, s2_ref, e1_ref, e2_ref, *, heads, tm):
    qb = q_ref[...].astype(BF16)
    pad = 56 - len(_CANDS)
    neg = jnp.full((pad, tm), -jnp.inf, F32)
    cw = jnp.concatenate([jnp.ones((len(_CANDS), tm), F32), jnp.zeros((pad, tm), F32)], axis=0)
    for h in range(heads):
        s = [_dot_nt(keys_ref[2 * h + c], qb[:, 128 * (2 * h + c):128 * (2 * h + c + 1)]) for c in range(2)]
        top = [[x[0:1, :] for x in _top16_rows(s[c])] for c in range(2)]
        cv = jnp.concatenate([top[0][i] + top[1][j] for i, j in _CANDS] + [neg], axis=0)
        tau = jnp.full((1, tm), -jnp.inf, F32)
        seen = jnp.zeros((1, tm), F32)
        for m, cnt in _top_distinct(cv, cw, PEER_TOPK):
            seen = seen + cnt
            tau = jnp.maximum(tau, jnp.where(seen >= PEER_TOPK, m, -jnp.inf))
        m1, m2 = top[0][0], top[1][0]
        z = jnp.sum(jnp.where(cv >= tau, cw * jnp.exp(cv - (m1 + m2)), 0.0), axis=0, keepdims=True)
        e1 = jnp.exp(s[0] - m1) / z
        e2 = jnp.exp(s[1] - m2)
        second = jnp.concatenate(top[1], axis=0)
        theta = [jnp.min(jnp.where(second + top[0][i] >= tau, second, jnp.inf), axis=0, keepdims=True)
                 for i in range(PEER_TOPK)]
        first = jnp.concatenate(top[0], axis=0)
        theta = jnp.concatenate(theta, axis=0)
        for lb in range(tm // 128):
            lanes = slice(128 * lb, 128 * lb + 128)
            s1_ref[lb, h] = s[0][:, lanes]
            t1_ref[lb, h] = first[:, lanes]
            th_ref[lb, h] = theta[:, lanes]
            s2_ref[lb, h] = s[1][:, lanes]
            e1_ref[lb, h] = e1[:, lanes]
            e2_ref[lb, h] = e2[:, lanes]


def _route(q, keys, heads, tm):
    t_all = q.shape[0]
    big = pl.BlockSpec((tm // 128, heads, 128, 128), lambda i: (i, 0, 0, 0))
    top = pl.BlockSpec((tm // 128, heads, PEER_TOPK, 128), lambda i: (i, 0, 0, 0))
    rows = [128, PEER_TOPK, PEER_TOPK, 128, 128, 128]
    return pl.pallas_call(
        functools.partial(_route_kernel, heads=heads, tm=tm),
        grid=(t_all // tm,),
        in_specs=[pl.BlockSpec((tm, q.shape[1]), lambda i: (i, 0)), _full(keys.shape)],
        out_specs=[big if r == 128 else top for r in rows],
        out_shape=[jax.ShapeDtypeStruct((t_all // 128, heads, r, 128), F32) for r in rows],
        compiler_params=_params("parallel"),
        name="peer_route",
    )(q, keys)


GATE_GROUP = 2


def _peer_kernel(t_ref, u_ref, vt_ref, s1_ref, t1_ref, theta_ref, s2_ref, e1_ref, e2_ref, h2_ref, gf_ref,
                 y_ref, acc_ref, gate_ref, th_ref, *, heads, te):
    j = pl.program_id(1)
    tm = t_ref.shape[0]
    n_a = te // 128
    assert n_a == 8

    def thresholds(step, slot):
        a_rows = pl.ds(pl.multiple_of(step * n_a, 8), 8)
        for lb in range(tm // 128):
            for h in range(heads):
                s1 = s1_ref[lb, h, a_rows, :]
                th = jnp.full(s1.shape, jnp.inf, F32)
                for i in range(PEER_TOPK):
                    th = jnp.minimum(th, jnp.where(s1 >= t1_ref[lb, h, i:i + 1, :], theta_ref[lb, h, i:i + 1, :], jnp.inf))
                th_ref[slot, lb, h] = th

    @pl.when(j == 0)
    def _():
        acc_ref[...] = jnp.zeros_like(acc_ref)
        thresholds(0, 0)

    tile = (8, 128)
    slot = j % 2
    for a8 in range(0, n_a, 8):
        a_rows = pl.ds(pl.multiple_of(j * n_a + a8, 8), 8)
        for ag in range(0, 8, GATE_GROUP):

            def lane_block(lb, carry, a_rows=a_rows, ag=ag, a8=a8):
                acc = [[jnp.zeros(tile, F32) for _ in range(16)] for _ in range(GATE_GROUP)]
                for h in range(heads):
                    th = th_ref[slot, lb, h]
                    e1 = e1_ref[lb, h, a_rows, :]
                    thb = [jnp.broadcast_to(th[ag + ai:ag + ai + 1, :], tile) for ai in range(GATE_GROUP)]
                    e1b = [jnp.broadcast_to(e1[ag + ai:ag + ai + 1, :], tile) for ai in range(GATE_GROUP)]
                    for r in range(16):
                        s2 = s2_ref[lb, h, 8 * r:8 * r + 8, :]
                        e2 = e2_ref[lb, h, 8 * r:8 * r + 8, :]
                        for ai in range(GATE_GROUP):
                            acc[ai][r] = acc[ai][r] + jnp.where(s2 >= thb[ai], e2, 0.0) * e1b[ai]
                for ai in range(GATE_GROUP):
                    for r in range(16):
                        row = (a8 + ag + ai) * 128 + 8 * r
                        gate_ref[lb, row:row + 8, :] = acc[ai][r]
                return carry

            lax.fori_loop(0, tm // 128, lane_block, 0)

    hp = _dot_nt(u_ref[...], t_ref[...])
    thresholds(jnp.minimum(j + 1, pl.num_programs(1) - 1), 1 - slot)
    gate = jnp.concatenate([gate_ref[lb] for lb in range(tm // 128)], axis=1)
    act = 0.5 * hp * (1.0 + lax.erf(hp * (1.0 / math.sqrt(2.0)))) * gate
    acc_ref[...] += _dot(vt_ref[...], act.astype(BF16))

    @pl.when(j == pl.num_programs(1) - 1)
    def _():
        y_ref[...] = _rms(h2_ref[...] + acc_ref[...].T, gf_ref[...])


def _peer(t, u, vt, s1, t1, theta, s2, e1, e2, h2, gf, heads, tm, te):
    t_all, d = t.shape
    n_exp = u.shape[0]
    tok = pl.BlockSpec((tm // 128, heads, 128, 128), lambda i, j: (i, 0, 0, 0))
    top = pl.BlockSpec((tm // 128, heads, PEER_TOPK, 128), lambda i, j: (i, 0, 0, 0))
    return pl.pallas_call(
        functools.partial(_peer_kernel, heads=heads, te=te),
        grid=(t_all // tm, n_exp // te),
        in_specs=[pl.BlockSpec((tm, d), lambda i, j: (i, 0)),
                  pl.BlockSpec((te, d), lambda i, j: (j, 0)),
                  pl.BlockSpec((d, te), lambda i, j: (0, j)),
                  tok, top, top, tok, tok, tok,
                  pl.BlockSpec((tm, d), lambda i, j: (i, 0)),
                  _full(gf.shape)],
        out_specs=pl.BlockSpec((tm, d), lambda i, j: (i, 0)),
        out_shape=jax.ShapeDtypeStruct((t_all, d), F32),
        scratch_shapes=[pltpu.VMEM((d, tm), F32), pltpu.VMEM((tm // 128, te, 128), F32),
                        pltpu.VMEM((2, tm // 128, heads, 8, 128), F32)],
        compiler_params=_params("parallel", "arbitrary"),
        name="peer_experts",
    )(t, u, vt, s1, t1, theta, s2, e1, e2, h2, gf)


def _tile(n, cap):
    t = cap
    while n % t:
        t //= 2
    return t


def kernel(x, meta_tokens, norm_mix_g, w_in, mla_q_norm_g, mla_w_uq, mla_kv_norm_g, mla_w_ukv, mla_out_g, rwkv_mu, rwkv_w0, rwkv_w_up, rwkv_a0, rwkv_a_up, rwkv_g_up, rwkv_k_k, rwkv_k_a, rwkv_r_k, rwkv_ln_w, rwkv_ln_b, w_out, norm_ffn_g, peer_w_q, peer_sub_keys, peer_u, peer_v, norm_final_g):
    batch, s_len, d = x.shape
    assert w_in.shape[0] == 1, "one layer"
    assert s_len % ATT_Q == 0 and d == 1024
    lp = X0 + s_len
    tp = batch * lp
    row = lambda a: a.reshape(1, -1).astype(F32)

    head = jnp.concatenate([jnp.zeros((META0, d), F32), meta_tokens.astype(F32)], axis=0)

    wi = w_in[0]
    kr = wi[:, 384:448]
    z64 = jnp.zeros((d, 64), F32)
    w_ext = jnp.concatenate([wi[:, 0:384], kr, z64, kr[:, 32:], kr[:, :32], z64, wi[:, 448:]], axis=1).astype(BF16)
    n_mla = 640
    z_up = jnp.zeros((64, 512), F32)
    wup = jnp.concatenate([rwkv_w_up[0], z_up], axis=0).astype(BF16)
    aup = jnp.concatenate([z_up, rwkv_a_up[0]], axis=0).astype(BF16)
    head_id = jnp.arange(512) // RWKV_HEAD
    bd = (head_id[:, None] == head_id[None, :]).astype(BF16)
    rwkv = (row(rwkv_mu[0]), row(rwkv_w0[0]), row(rwkv_a0[0]), row(rwkv_k_k[0]), row(rwkv_k_a[0]),
            row(rwkv_r_k[0]), wup, aup, rwkv_g_up[0].astype(BF16), bd)
    pm, r, lw, k2, vv, kn, bv, bonus, g = _inproj(head, x, row(norm_mix_g[0]), w_ext, rwkv, n_mla, _tile(s_len, 512))

    pos = jnp.maximum(jnp.arange(lp) - META0, 0).astype(F32)
    inv = ROPE_THETA ** (-jnp.arange(32, dtype=F32) / 32)
    ang = pos[:, None] * inv[None, :]
    zl = jnp.zeros((lp, 64), F32)
    cos = jnp.tile(jnp.concatenate([jnp.cos(ang), jnp.cos(ang), zl], axis=1), (batch, 1))
    sin = jnp.tile(jnp.concatenate([-jnp.sin(ang), jnp.sin(ang), zl], axis=1), (batch, 1))

    wuq = mla_w_uq[0].reshape(256, MLA_HEADS, 192)
    zq = jnp.zeros((256, MLA_HEADS, 64), F32)
    wqa = jnp.concatenate([wuq, zq], axis=2).reshape(256, MLA_HEADS * 256).astype(BF16)
    wqb = jnp.concatenate([wuq[:, :, 160:192], wuq[:, :, 128:160], zq], axis=2).reshape(256, MLA_HEADS * 128).astype(BF16)
    wukv = mla_w_ukv[0].reshape(128, MLA_HEADS, 256)
    wk = wukv[:, :, :128].reshape(128, MLA_HEADS * 128).astype(BF16)
    wv = wukv[:, :, 128:].reshape(128, MLA_HEADS * 128).astype(BF16)
    q, k, v = _mla_prep(pm, cos, sin, row(mla_q_norm_g[0]), row(mla_kv_norm_g[0]), wqa, wqb, wk, wv,
                        192 ** -0.5, _tile(tp, 512))
    o_mla = _attention(q, k, v, batch, lp, s_len)

    o_rwkv = _rwkv_scan(r, lw, k2, vv, kn, bv, batch, lp, s_len)

    heads = peer_sub_keys.shape[1]
    h2, t, pq = _outproj(x.reshape(batch * s_len, d), o_mla, o_rwkv, bonus, g, row(mla_out_g[0]),
                         row(rwkv_ln_w[0]), row(rwkv_ln_b[0]), bd, w_out[0].astype(BF16), row(norm_ffn_g[0]),
                         peer_w_q[0].astype(BF16), batch, lp, s_len, _tile(s_len, 512))

    t_all = batch * s_len
    keys = peer_sub_keys[0].reshape(heads * 2, 128, 128).astype(BF16)
    tm = _tile(t_all, 512)
    s1, t1, theta, s2, e1, e2 = _route(pq, keys, heads, tm)
    y = _peer(t, peer_u[0].astype(BF16), peer_v[0].T.astype(BF16), s1, t1, theta, s2, e1, e2, h2,
              row(norm_final_g), heads, tm, 1024)
    return y.reshape(batch, s_len, d)
```

```python
import functools
import math

import jax
import jax.numpy as jnp
from jax import lax
from jax.experimental import pallas as pl
from jax.experimental.pallas import tpu as pltpu

F32 = jnp.float32
BF16 = jnp.bfloat16
HIGHEST = lax.Precision.HIGHEST

RMS_EPS = 1e-6
GN_EPS = 64e-5
ROPE_THETA = 10000.0
N_META = 16
CHUNK = 64
X0 = 512
META0 = X0 - N_META
K0 = X0 - 128
SCAN_C0 = (X0 - CHUNK) // CHUNK
MLA_HEADS = 4
RWKV_HEAD = 64
PEER_TOPK = 16
VMEM_LIMIT = 56 * 1024 * 1024


def _dot(a, b, precision=None):
    return jnp.dot(a, b, preferred_element_type=F32, precision=precision)


def _dot_nt(a, b, precision=None):
    return lax.dot_general(a, b, (((1,), (1,)), ((), ())),
                           preferred_element_type=F32, precision=precision)


def _dot_split(x, m):
    hi = x.astype(BF16)
    lo = (x - hi.astype(F32)).astype(BF16)
    return _dot(hi, m) + _dot(lo, m)


def _rms(x, g):
    return x * lax.rsqrt(jnp.mean(x * x, axis=-1, keepdims=True) + RMS_EPS) * g


def _params(*sem):
    return pltpu.CompilerParams(dimension_semantics=sem, vmem_limit_bytes=VMEM_LIMIT)


def _full(shape):
    n = len(shape)
    return pl.BlockSpec(shape, lambda *_: (0,) * n)


def _softplus(x):
    return jnp.maximum(x, 0.0) + jnp.log(1.0 + jnp.exp(-jnp.abs(x)))


def _sigmoid(x):
    return 1.0 / (1.0 + jnp.exp(-x))


def _rwkv_prep(p, prev_ref, row0, params, outs):
    mu_ref, w0_ref, a0_ref, kk_ref, ka_ref, rk_ref, wup_ref, aup_ref, gup_ref, bd_ref = params
    r_ref, lw_ref, k_ref, v_ref, kn_ref, b_ref, bonus_ref, g_ref = outs
    tm = p.shape[0]
    rows = lax.broadcasted_iota(jnp.int32, (tm, 1), 0)
    shifted = jnp.where(rows == 0, prev_ref[7:8, :], pltpu.roll(p, 1, 0))
    prev_ref[...] = p[tm - 8:tm, :]
    pm = p + mu_ref[...] * (shifted - p)
    r, k, v = pm[:, 0:512], pm[:, 512:1024], pm[:, 1024:1536]
    dd, dg = pm[:, 1536:1664], pm[:, 1664:1792]
    w = -_softplus(-(w0_ref[...] + _dot(jnp.tanh(dd).astype(BF16), wup_ref[...]))) - 0.5
    log_decay = -jnp.exp(w)
    a = _sigmoid(a0_ref[...] + _dot(dd.astype(BF16), aup_ref[...]))
    g_ref[...] = _dot(_sigmoid(dg).astype(BF16), gup_ref[...])
    bd = bd_ref[...]
    kk = k * kk_ref[...]
    norm = jnp.sqrt(_dot_split(kk * kk, bd))
    kk = kk / jnp.maximum(norm, 1e-12)
    k2 = k * (1.0 + (a - 1.0) * ka_ref[...])
    real = (row0 + rows) >= META0
    r_ref[...] = r
    lw_ref[...] = jnp.where(real, log_decay, 0.0)
    k_ref[...] = k2
    v_ref[...] = v
    kn_ref[...] = kk
    b_ref[...] = kk * a
    bonus_ref[...] = _dot_split(r * k2 * rk_ref[...], bd) * v


def _inproj_kernel(head_ref, x_ref, g_ref, w_ref, mu_ref, w0_ref, a0_ref, kk_ref, ka_ref, rk_ref,
                   wup_ref, aup_ref, gup_ref, bd_ref,
                   pm_ref, r_ref, lw_ref, k_ref, v_ref, kn_ref, b_ref, bonus_ref, gate_ref, prev_ref,
                   *, n_mla, n_head, tm):
    j = pl.program_id(1)

    @pl.when(j == 0)
    def _():
        prev_ref[...] = jnp.zeros_like(prev_ref)

    def project(rows):
        n = _rms(rows, g_ref[...]).astype(BF16)
        p = _dot(n, w_ref[...])
        pm_ref[...] = p[:, :n_mla]
        _rwkv_prep(p[:, n_mla:], prev_ref, j * tm,
                   (mu_ref, w0_ref, a0_ref, kk_ref, ka_ref, rk_ref, wup_ref, aup_ref, gup_ref, bd_ref),
                   (r_ref, lw_ref, k_ref, v_ref, kn_ref, b_ref, bonus_ref, gate_ref))

    @pl.when(j < n_head)
    def _():
        project(head_ref[...])

    @pl.when(j >= n_head)
    def _():
        project(x_ref[...])


def _inproj(head, x, g, w, rwkv, n_mla, tm):
    batch, s_len, d = x.shape
    n_head = head.shape[0] // tm
    nl = n_head + s_len // tm
    rows = batch * nl * tm
    out = lambda width: pl.BlockSpec((tm, width), lambda b, j: (b * nl + j, 0))
    return pl.pallas_call(
        functools.partial(_inproj_kernel, n_mla=n_mla, n_head=n_head, tm=tm),
        grid=(batch, nl),
        in_specs=[pl.BlockSpec((tm, d), lambda b, j: (jnp.minimum(j, n_head - 1), 0)),
                  pl.BlockSpec((None, tm, d), lambda b, j: (b, jnp.maximum(j - n_head, 0), 0)),
                  _full((1, d)), _full(w.shape)] + [_full(a.shape) for a in rwkv],
        out_specs=[out(n_mla)] + [out(512)] * 8,
        out_shape=[jax.ShapeDtypeStruct((rows, n_mla), F32)] + [jax.ShapeDtypeStruct((rows, 512), F32)] * 8,
        scratch_shapes=[pltpu.VMEM((8, w.shape[1] - n_mla), F32)],
        compiler_params=_params("parallel", "arbitrary"),
        name="inproj",
    )(head, x, g, w, *rwkv)


def _mla_prep_kernel(pm_ref, cos_ref, sin_ref, gq_ref, gkv_ref, wqa_ref, wqb_ref, wk_ref, wv_ref,
                     q_ref, k_ref, v_ref, *, scale):
    pm = pm_ref[...]
    nq = _rms(pm[:, 0:256], gq_ref[...]).astype(BF16)
    nkv = _rms(pm[:, 256:384], gkv_ref[...]).astype(BF16)
    cos, sin = cos_ref[...], sin_ref[...]
    qa = _dot(nq, wqa_ref[...])
    qb = _dot(nq, wqb_ref[...])
    kn = _dot(nkv, wk_ref[...])
    vv = _dot(nkv, wv_ref[...])
    kr = pm[:, 384:512] * cos + pm[:, 512:640] * sin
    for h in range(MLA_HEADS):
        qn = qa[:, 256 * h:256 * h + 128]
        qr = qa[:, 256 * h + 128:256 * h + 256] * cos + qb[:, 128 * h:128 * h + 128] * sin
        q_ref[h] = (jnp.concatenate([qn, qr], axis=-1) * scale).astype(BF16)
        k_ref[h] = jnp.concatenate([kn[:, 128 * h:128 * h + 128], kr], axis=-1).astype(BF16)
        v_ref[h] = vv[:, 128 * h:128 * h + 128].astype(BF16)


def _mla_prep(pm, cos, sin, gq, gkv, wqa, wqb, wk, wv, scale, tm):
    tp = pm.shape[0]
    row = lambda w: pl.BlockSpec((tm, w), lambda i: (i, 0))
    head = lambda w: pl.BlockSpec((MLA_HEADS, tm, w), lambda i: (0, i, 0))
    return pl.pallas_call(
        functools.partial(_mla_prep_kernel, scale=scale),
        grid=(tp // tm,),
        in_specs=[row(pm.shape[1]), row(128), row(128), _full(gq.shape), _full(gkv.shape),
                  _full(wqa.shape), _full(wqb.shape), _full(wk.shape), _full(wv.shape)],
        out_specs=[head(256), head(256), head(128)],
        out_shape=[jax.ShapeDtypeStruct((MLA_HEADS, tp, 256), BF16),
                   jax.ShapeDtypeStruct((MLA_HEADS, tp, 256), BF16),
                   jax.ShapeDtypeStruct((MLA_HEADS, tp, 128), BF16)],
        compiler_params=_params("parallel"),
        name="mla_prep",
    )(pm, cos, sin, gq, gkv, wqa, wqb, wk, wv)


ATT_Q = 256


ATT_HEADS = 2


def _attn_kernel(q_ref, k_ref, v_ref, o_ref, *, n_q):
    heads = range(q_ref.shape[0])
    for qi in range(n_q):
        q0 = X0 + ATT_Q * qi
        kend = q0 + ATT_Q
        n = kend - K0
        kk = lax.broadcasted_iota(jnp.int32, (1, n), 1)
        ck = jnp.where(kk < 128 - N_META, 1 << 20, kk >> 6)
        cq = ((ATT_Q * qi + lax.broadcasted_iota(jnp.int32, (ATT_Q, 1), 0)) >> 6) + 2
        visible = ck <= cq
        s = [_dot_nt(q_ref[h, q0:q0 + ATT_Q, :], k_ref[h, K0:kend, :]) for h in heads]
        s = [jnp.where(visible, x, -1e30) for x in s]
        e = [jnp.exp(x - jnp.max(x, axis=-1, keepdims=True)) for x in s]
        l = [jnp.sum(x, axis=-1, keepdims=True) for x in e]
        o = [_dot(x.astype(BF16), v_ref[h, K0:kend, :]) for h, x in zip(heads, e)]
        for h in heads:
            o_ref[ATT_Q * qi:ATT_Q * (qi + 1), 128 * h:128 * (h + 1)] = o[h] / l[h]


def _attention(q, k, v, batch, lp, s_len):
    return pl.pallas_call(
        functools.partial(_attn_kernel, n_q=s_len // ATT_Q),
        grid=(batch, MLA_HEADS // ATT_HEADS),
        in_specs=[pl.BlockSpec((ATT_HEADS, lp, 256), lambda b, h: (h, b, 0)),
                  pl.BlockSpec((ATT_HEADS, lp, 256), lambda b, h: (h, b, 0)),
                  pl.BlockSpec((ATT_HEADS, lp, 128), lambda b, h: (h, b, 0))],
        out_specs=pl.BlockSpec((s_len, 128 * ATT_HEADS), lambda b, h: (b, h)),
        out_shape=jax.ShapeDtypeStruct((batch * s_len, MLA_HEADS * 128), F32),
        compiler_params=_params("parallel", "parallel"),
        name="mla_attn",
    )(q, k, v)


def _bdot(a, b):
    return _dot(a.astype(BF16), b.astype(BF16))


def _rwkv_scan_kernel(r_ref, lw_ref, k_ref, v_ref, kn_ref, b_ref, o_ref, z_ref):
    c = CHUNK
    w = 4 * c

    @pl.when(pl.program_id(1) == 0)
    def _():
        z_ref[...] = jnp.zeros_like(z_ref)

    ri = lax.broadcasted_iota(jnp.int32, (w, w), 0)
    ci = lax.broadcasted_iota(jnp.int32, (w, w), 1)
    same_head = (ri >> 6) == (ci >> 6)
    strict = (ri & 63) > (ci & 63)
    incl = (ri & 63) >= (ci & 63)
    eye = ri == ci
    tri = (lax.broadcasted_iota(jnp.int32, (c, c), 0) >= lax.broadcasted_iota(jnp.int32, (c, c), 1)).astype(F32)

    def stack(x):
        return jnp.where(same_head, jnp.concatenate([x, x, x, x], axis=0), 0.0)

    n_seq = lw_ref.shape[0]
    cum_all = [_dot(tri, lw_ref[p], HIGHEST) for p in range(n_seq)]
    groups = range(z_ref.shape[0])
    st = [dict() for _ in groups]
    for q in groups:
        p = q // 2
        lanes = slice(w * (q % 2), w * (q % 2) + w)
        lw = lw_ref[p, :, lanes]
        cum = cum_all[p][:, lanes]
        cum_prev = cum - lw
        mid = cum[c // 2 - 1:c // 2, :]
        tot = cum[c - 1:c, :]
        a = -kn_ref[p, :, lanes]
        b = b_ref[p, :, lanes]
        k = k_ref[p, :, lanes]
        r = r_ref[p, :, lanes]
        dn = jnp.exp(mid - cum)
        de = jnp.exp(tot - cum)
        a_t = stack(a * jnp.exp(cum_prev - mid))
        b_t = stack(b * dn)
        k_t = stack(k * dn)
        r_t = stack(r * jnp.exp(cum - mid))
        st[q].update(seq=p, lanes=lanes, tot=tot, a_0=stack(a * jnp.exp(cum_prev)), r_0=stack(r * jnp.exp(cum)),
                     b_e=stack(b * de), k_e=stack(k * de), vs=stack(v_ref[p, :, lanes]),
                     lhs=jnp.concatenate([a_t, r_t], axis=0).astype(BF16),
                     rhs=jnp.concatenate([b_t, k_t], axis=0).astype(BF16))
    for q in groups:
        inter = _dot_nt(st[q]["lhs"], st[q]["rhs"])
        l_ab = jnp.where(strict, inter[:w, :w], 0.0)
        st[q].update(l_ak=jnp.where(strict, inter[:w, w:], 0.0), m_rb=jnp.where(incl, inter[w:, :w], 0.0),
                     m_rk=jnp.where(incl, inter[w:, w:], 0.0),
                     t_inv=jnp.where(eye, 1.0, 0.0) + l_ab, l_ab=l_ab)
    for q in groups:
        st[q]["power"] = _bdot(st[q]["l_ab"], st[q]["l_ab"])
    for _ in range(int(math.log2(c)) - 2):
        for q in groups:
            both = _bdot(st[q]["power"], jnp.concatenate([st[q]["power"], st[q]["t_inv"]], axis=1))
            st[q]["power"], st[q]["t_inv"] = both[:, :w], st[q]["t_inv"] + both[:, w:]
    for q in groups:
        st[q]["t_inv"] = st[q]["t_inv"] + _bdot(st[q]["power"], st[q]["t_inv"])
    for q in groups:
        st[q]["pq"] = _bdot(st[q]["t_inv"], jnp.concatenate([st[q]["a_0"], st[q]["l_ak"]], axis=1))
    for q in groups:
        st[q]["mix"] = _bdot(jnp.concatenate([st[q]["m_rb"], st[q]["b_e"].T], axis=0), st[q]["pq"])
    for q in groups:
        mix, tot = st[q]["mix"], st[q]["tot"]
        r_hat = st[q]["r_0"] + mix[:w, :w]
        m_o = mix[:w, w:] + st[q]["m_rk"]
        g = jnp.where(eye, jnp.exp(tot), 0.0) + mix[w:, :w]
        m_h = mix[w:, w:] + st[q]["k_e"].T
        lhs = jnp.concatenate([jnp.concatenate([r_hat, m_o], axis=1), jnp.concatenate([g, m_h], axis=1)], axis=0)
        new = _bdot(lhs, jnp.concatenate([z_ref[q], st[q]["vs"]], axis=0))
        o_rows = new[:w]
        o_ref[st[q]["seq"], :, st[q]["lanes"]] = o_rows[0:c] + o_rows[c:2 * c] + o_rows[2 * c:3 * c] + o_rows[3 * c:4 * c]
        z_ref[q] = new[w:]


SCAN_SEQS = 2


def _rwkv_scan(r, lw, k, v, kn, b, batch, lp, s_len):
    nc = s_len // CHUNK + 1
    assert batch % SCAN_SEQS == 0
    pairs = batch // SCAN_SEQS
    ins = [a.reshape(pairs, SCAN_SEQS, lp, 512) for a in (r, lw, k, v, kn, b)]
    blk = pl.BlockSpec((None, SCAN_SEQS, CHUNK, 512), lambda bi, ci: (bi, 0, SCAN_C0 + ci, 0))
    out = pl.pallas_call(
        _rwkv_scan_kernel,
        grid=(pairs, nc),
        in_specs=[blk] * 6,
        out_specs=pl.BlockSpec((None, SCAN_SEQS, CHUNK, 512), lambda bi, ci: (bi, 0, jnp.maximum(ci - 1, 0), 0)),
        out_shape=jax.ShapeDtypeStruct((pairs, SCAN_SEQS, s_len, 512), F32),
        scratch_shapes=[pltpu.VMEM((2 * SCAN_SEQS, 256, 256), F32)],
        compiler_params=_params("parallel", "arbitrary"),
        name="rwkv_scan",
    )(*ins)
    return out.reshape(batch * s_len, 512)


def _outproj_kernel(x_ref, om_ref, orw_ref, bonus_ref, g_ref, og_ref, lnw_ref, lnb_ref, bd_ref,
                    wout_ref, gffn_ref, wq_ref, h2_ref, t_ref, q_ref):
    y_mla = _rms(om_ref[...], og_ref[...])
    o = orw_ref[...]
    bd = bd_ref[...]
    mean = _dot_split(o, bd) * (1.0 / RWKV_HEAD)
    cen = o - mean
    var = _dot_split(cen * cen, bd) * (1.0 / RWKV_HEAD)
    y_rwkv = (cen * lax.rsqrt(var + GN_EPS) * lnw_ref[...] + lnb_ref[...] + bonus_ref[...]) * g_ref[...]
    y = jnp.concatenate([y_mla, y_rwkv], axis=-1).astype(BF16)
    h2 = x_ref[...] + _dot(y, wout_ref[...])
    h2_ref[...] = h2
    t = _rms(h2, gffn_ref[...]).astype(BF16)
    t_ref[...] = t
    q_ref[...] = _dot(t, wq_ref[...])


def _outproj(x, o_mla, o_rwkv, bonus, g, og, lnw, lnb, bd, wout, gffn, wq, batch, lp, s_len, tm):
    t_all, d = x.shape
    ns = s_len // tm
    npad = lp // tm
    xrow = lambda w: pl.BlockSpec((tm, w), lambda b, j: (b * ns + j, 0))
    prow = lambda w: pl.BlockSpec((tm, w), lambda b, j: (b * npad + X0 // tm + j, 0))
    vec = lambda a: _full(a.shape)
    return pl.pallas_call(
        _outproj_kernel,
        grid=(batch, ns),
        in_specs=[xrow(d), xrow(512), xrow(512), prow(512), prow(512), vec(og), vec(lnw), vec(lnb), vec(bd),
                  vec(wout), vec(gffn), vec(wq)],
        out_specs=[xrow(d), xrow(d), xrow(wq.shape[1])],
        out_shape=[jax.ShapeDtypeStruct((t_all, d), F32), jax.ShapeDtypeStruct((t_all, d), BF16),
                   jax.ShapeDtypeStruct((t_all, wq.shape[1]), F32)],
        compiler_params=_params("parallel", "parallel"),
        name="outproj",
    )(x, o_mla, o_rwkv, bonus, g, og, lnw, lnb, bd, wout, gffn, wq)


_CANDS = [(i, j) for i in range(PEER_TOPK) for j in range(PEER_TOPK) if (i + 1) * (j + 1) <= PEER_TOPK]


def _exchange(v, i, l):
    v[i], v[l] = jnp.maximum(v[i], v[l]), jnp.minimum(v[i], v[l])


def _bitonic_merge_desc(v):
    n = len(v)
    j = n // 2
    while j >= 1:
        for i in range(n):
            if i ^ j > i:
                _exchange(v, i, i ^ j)
        j //= 2
    return v


def _bitonic_sort_desc(v):
    n = len(v)
    k = 2
    while k <= n:
        j = k // 2
        while j >= 1:
            for i in range(n):
                l = i ^ j
                if l > i:
                    if (i & k) == 0 or k == n:
                        _exchange(v, i, l)
                    else:
                        _exchange(v, l, i)
            j //= 2
        k *= 2
    return v


def _top16_rows(s):
    v = _bitonic_sort_desc([s[8 * k:8 * k + 8, :] for k in range(PEER_TOPK)])
    for shift in (4, 2, 1):
        other = [pltpu.roll(x, shift, 0) for x in v]
        v = _bitonic_merge_desc([jnp.maximum(v[i], other[PEER_TOPK - 1 - i]) for i in range(PEER_TOPK)])
    return v


def _top_distinct(x, weight, n):
    out = []
    for _ in range(n):
        m = jnp.max(x, axis=0, keepdims=True)
        eq = x == m
        cnt = jnp.sum(jnp.where(eq, weight, 0.0), axis=0, keepdims=True)
        x = jnp.where(eq, -jnp.inf, x)
        out.append((m, cnt))
    return out


def _route_kernel(q_ref, keys_ref, s1_ref, t1_ref, th_ref, s2_ref, e1_ref, e2_ref, *, heads, tm):
    qb = q_ref[...].astype(BF16)
    pad = 56 - len(_CANDS)
    neg = jnp.full((pad, tm), -jnp.inf, F32)
    cw = jnp.concatenate([jnp.ones((len(_CANDS), tm), F32), jnp.zeros((pad, tm), F32)], axis=0)
    for h in range(heads):
        s = [_dot_nt(keys_ref[2 * h + c], qb[:, 128 * (2 * h + c):128 * (2 * h + c + 1)]) for c in range(2)]
        top = [[x[0:1, :] for x in _top16_rows(s[c])] for c in range(2)]
        cv = jnp.concatenate([top[0][i] + top[1][j] for i, j in _CANDS] + [neg], axis=0)
        tau = jnp.full((1, tm), -jnp.inf, F32)
        seen = jnp.zeros((1, tm), F32)
        for m, cnt in _top_distinct(cv, cw, PEER_TOPK):
            seen = seen + cnt
            tau = jnp.maximum(tau, jnp.where(seen >= PEER_TOPK, m, -jnp.inf))
        m1, m2 = top[0][0], top[1][0]
        z = jnp.sum(jnp.where(cv >= tau, cw * jnp.exp(cv - (m1 + m2)), 0.0), axis=0, keepdims=True)
        e1 = jnp.exp(s[0] - m1) * (1.0 / z)
        e2 = jnp.exp(s[1] - m2)
        second = jnp.concatenate(top[1], axis=0)
        theta = [jnp.min(jnp.where(second + top[0][i] >= tau, second, jnp.inf), axis=0, keepdims=True)
                 for i in range(PEER_TOPK)]
        first = jnp.concatenate(top[0], axis=0)
        theta = jnp.concatenate(theta, axis=0)
        for lb in range(tm // 128):
            lanes = slice(128 * lb, 128 * lb + 128)
            s1_ref[lb, h] = s[0][:, lanes]
            t1_ref[lb, h] = first[:, lanes]
            th_ref[lb, h] = theta[:, lanes]
            s2_ref[lb, h] = s[1][:, lanes]
            e1_ref[lb, h] = e1[:, lanes]
            e2_ref[lb, h] = e2[:, lanes]


def _route(q, keys, heads, tm):
    t_all = q.shape[0]
    big = pl.BlockSpec((tm // 128, heads, 128, 128), lambda i: (i, 0, 0, 0))
    top = pl.BlockSpec((tm // 128, heads, PEER_TOPK, 128), lambda i: (i, 0, 0, 0))
    rows = [128, PEER_TOPK, PEER_TOPK, 128, 128, 128]
    return pl.pallas_call(
        functools.partial(_route_kernel, heads=heads, tm=tm),
        grid=(t_all // tm,),
        in_specs=[pl.BlockSpec((tm, q.shape[1]), lambda i: (i, 0)), _full(keys.shape)],
        out_specs=[big if r == 128 else top for r in rows],
        out_shape=[jax.ShapeDtypeStruct((t_all // 128, heads, r, 128), F32) for r in rows],
        compiler_params=_params("parallel"),
        name="peer_route",
    )(q, keys)


GATE_GROUP = 2


def _peer_kernel(t_ref, u_ref, vt_ref, s1_ref, t1_ref, theta_ref, s2_ref, e1_ref, e2_ref, h2_ref, gf_ref,
                 y_ref, acc_ref, gate_ref, th_ref, *, heads, te):
    j = pl.program_id(1)
    tm = t_ref.shape[0]
    n_a = te // 128
    assert n_a == 8

    def thresholds(step, slot):
        a_rows = pl.ds(pl.multiple_of(step * n_a, 8), 8)
        for lb in range(tm // 128):
            for h in range(heads):
                s1 = s1_ref[lb, h, a_rows, :]
                th = jnp.full(s1.shape, jnp.inf, F32)
                for i in range(PEER_TOPK):
                    th = jnp.minimum(th, jnp.where(s1 >= t1_ref[lb, h, i:i + 1, :], theta_ref[lb, h, i:i + 1, :], jnp.inf))
                th_ref[slot, lb, h] = th

    @pl.when(j == 0)
    def _():
        acc_ref[...] = jnp.zeros_like(acc_ref)
        thresholds(0, 0)

    tile = (8, 128)
    slot = j % 2
    for a8 in range(0, n_a, 8):
        a_rows = pl.ds(pl.multiple_of(j * n_a + a8, 8), 8)
        for ag in range(0, 8, GATE_GROUP):

            def lane_block(lb, carry, a_rows=a_rows, ag=ag, a8=a8):
                acc = [[jnp.zeros(tile, F32) for _ in range(16)] for _ in range(GATE_GROUP)]
                for h in range(heads):
                    th = th_ref[slot, lb, h]
                    e1 = e1_ref[lb, h, a_rows, :]
                    thb = [jnp.broadcast_to(th[ag + ai:ag + ai + 1, :], tile) for ai in range(GATE_GROUP)]
                    e1b = [jnp.broadcast_to(e1[ag + ai:ag + ai + 1, :], tile) for ai in range(GATE_GROUP)]
                    for r in range(16):
                        s2 = s2_ref[lb, h, 8 * r:8 * r + 8, :]
                        e2 = e2_ref[lb, h, 8 * r:8 * r + 8, :]
                        for ai in range(GATE_GROUP):
                            acc[ai][r] = acc[ai][r] + jnp.where(s2 >= thb[ai], e2, 0.0) * e1b[ai]
                for ai in range(GATE_GROUP):
                    for r in range(16):
                        row = (a8 + ag + ai) * 128 + 8 * r
                        gate_ref[lb, row:row + 8, :] = acc[ai][r]
                return carry

            lax.fori_loop(0, tm // 128, lane_block, 0)

    hp = _dot_nt(u_ref[...], t_ref[...])
    thresholds(jnp.minimum(j + 1, pl.num_programs(1) - 1), 1 - slot)
    gate = jnp.concatenate([gate_ref[lb] for lb in range(tm // 128)], axis=1)
    act = 0.5 * hp * (1.0 + lax.erf(hp * (1.0 / math.sqrt(2.0)))) * gate
    acc_ref[...] += _dot(vt_ref[...], act.astype(BF16))

    @pl.when(j == pl.num_programs(1) - 1)
    def _():
        y_ref[...] = _rms(h2_ref[...] + acc_ref[...].T, gf_ref[...])


def _peer(t, u, vt, s1, t1, theta, s2, e1, e2, h2, gf, heads, tm, te):
    t_all, d = t.shape
    n_exp = u.shape[0]
    tok = pl.BlockSpec((tm // 128, heads, 128, 128), lambda i, j: (i, 0, 0, 0))
    top = pl.BlockSpec((tm // 128, heads, PEER_TOPK, 128), lambda i, j: (i, 0, 0, 0))
    return pl.pallas_call(
        functools.partial(_peer_kernel, heads=heads, te=te),
        grid=(t_all // tm, n_exp // te),
        in_specs=[pl.BlockSpec((tm, d), lambda i, j: (i, 0)),
                  pl.BlockSpec((te, d), lambda i, j: (j, 0)),
                  pl.BlockSpec((d, te), lambda i, j: (0, j)),
                  tok, top, top, tok, tok, tok,
                  pl.BlockSpec((tm, d), lambda i, j: (i, 0)),
                  _full(gf.shape)],
        out_specs=pl.BlockSpec((tm, d), lambda i, j: (i, 0)),
        out_shape=jax.ShapeDtypeStruct((t_all, d), F32),
        scratch_shapes=[pltpu.VMEM((d, tm), F32), pltpu.VMEM((tm // 128, te, 128), F32),
                        pltpu.VMEM((2, tm // 128, heads, 8, 128), F32)],
        compiler_params=_params("parallel", "arbitrary"),
        name="peer_experts",
    )(t, u, vt, s1, t1, theta, s2, e1, e2, h2, gf)


def _tile(n, cap):
    t = cap
    while n % t:
        t //= 2
    return t


def kernel(x, meta_tokens, norm_mix_g, w_in, mla_q_norm_g, mla_w_uq, mla_kv_norm_g, mla_w_ukv, mla_out_g, rwkv_mu, rwkv_w0, rwkv_w_up, rwkv_a0, rwkv_a_up, rwkv_g_up, rwkv_k_k, rwkv_k_a, rwkv_r_k, rwkv_ln_w, rwkv_ln_b, w_out, norm_ffn_g, peer_w_q, peer_sub_keys, peer_u, peer_v, norm_final_g):
    batch, s_len, d = x.shape
    assert w_in.shape[0] == 1, "one layer"
    assert s_len % ATT_Q == 0 and d == 1024
    lp = X0 + s_len
    tp = batch * lp
    row = lambda a: a.reshape(1, -1).astype(F32)

    head = jnp.concatenate([jnp.zeros((META0, d), F32), meta_tokens.astype(F32)], axis=0)

    wi = w_in[0]
    kr = wi[:, 384:448]
    z64 = jnp.zeros((d, 64), F32)
    w_ext = jnp.concatenate([wi[:, 0:384], kr, z64, kr[:, 32:], kr[:, :32], z64, wi[:, 448:]], axis=1).astype(BF16)
    n_mla = 640
    z_up = jnp.zeros((64, 512), F32)
    wup = jnp.concatenate([rwkv_w_up[0], z_up], axis=0).astype(BF16)
    aup = jnp.concatenate([z_up, rwkv_a_up[0]], axis=0).astype(BF16)
    head_id = jnp.arange(512) // RWKV_HEAD
    bd = (head_id[:, None] == head_id[None, :]).astype(BF16)
    rwkv = (row(rwkv_mu[0]), row(rwkv_w0[0]), row(rwkv_a0[0]), row(rwkv_k_k[0]), row(rwkv_k_a[0]),
            row(rwkv_r_k[0]), wup, aup, rwkv_g_up[0].astype(BF16), bd)
    pm, r, lw, k2, vv, kn, bv, bonus, g = _inproj(head, x, row(norm_mix_g[0]), w_ext, rwkv, n_mla, _tile(s_len, 512))

    pos = jnp.maximum(jnp.arange(lp) - META0, 0).astype(F32)
    inv = ROPE_THETA ** (-jnp.arange(32, dtype=F32) / 32)
    ang = pos[:, None] * inv[None, :]
    zl = jnp.zeros((lp, 64), F32)
    cos = jnp.tile(jnp.concatenate([jnp.cos(ang), jnp.cos(ang), zl], axis=1), (batch, 1))
    sin = jnp.tile(jnp.concatenate([-jnp.sin(ang), jnp.sin(ang), zl], axis=1), (batch, 1))

    wuq = mla_w_uq[0].reshape(256, MLA_HEADS, 192)
    zq = jnp.zeros((256, MLA_HEADS, 64), F32)
    wqa = jnp.concatenate([wuq, zq], axis=2).reshape(256, MLA_HEADS * 256).astype(BF16)
    wqb = jnp.concatenate([wuq[:, :, 160:192], wuq[:, :, 128:160], zq], axis=2).reshape(256, MLA_HEADS * 128).astype(BF16)
    wukv = mla_w_ukv[0].reshape(128, MLA_HEADS, 256)
    wk = wukv[:, :, :128].reshape(128, MLA_HEADS * 128).astype(BF16)
    wv = wukv[:, :, 128:].reshape(128, MLA_HEADS * 128).astype(BF16)
    q, k, v = _mla_prep(pm, cos, sin, row(mla_q_norm_g[0]), row(mla_kv_norm_g[0]), wqa, wqb, wk, wv,
                        192 ** -0.5, _tile(tp, 512))
    o_mla = _attention(q, k, v, batch, lp, s_len)

    o_rwkv = _rwkv_scan(r, lw, k2, vv, kn, bv, batch, lp, s_len)

    heads = peer_sub_keys.shape[1]
    h2, t, pq = _outproj(x.reshape(batch * s_len, d), o_mla, o_rwkv, bonus, g, row(mla_out_g[0]),
                         row(rwkv_ln_w[0]), row(rwkv_ln_b[0]), bd, w_out[0].astype(BF16), row(norm_ffn_g[0]),
                         peer_w_q[0].astype(BF16), batch, lp, s_len, _tile(s_len, 512))

    t_all = batch * s_len
    keys = peer_sub_keys[0].reshape(heads * 2, 128, 128).astype(BF16)
    tm = _tile(t_all, 512)
    s1, t1, theta, s2, e1, e2 = _route(pq, keys, heads, tm)
    y = _peer(t, peer_u[0].astype(BF16), peer_v[0].T.astype(BF16), s1, t1, theta, s2, e1, e2, h2,
              row(norm_final_g), heads, tm, 1024)
    return y.reshape(batch, s_len, d)
```
